```python
import math
import jax, jax.numpy as jnp
from jax import lax
import numpy as np

D_MODEL = 2048
BATCH = 1
SEQ = 8192
DEPTH = 1
DEC_BATCH = 4
DEC_SEQ = 8192
PAST_LEN = 128

N_HEADS = 16
QK_NOPE_DIM = 128
QK_ROPE_DIM = 64
V_HEAD_DIM = 128
Q_LORA_RANK = 512
KV_LORA_RANK = 512
ROPE_BASE = 10000.0
Q_BLOCK = 128
LRU_WIDTH = D_MODEL
LRU_BLOCKS = 16
LRU_BLOCK_W = LRU_WIDTH // LRU_BLOCKS
CONV_WIDTH = 4
CONV_LEFT = 2
LRU_C = 8.0
D_FF = ((8 * D_MODEL // 3 + 255) // 256) * 256
RMS_EPS = 1e-6
N_IN = Q_LORA_RANK + KV_LORA_RANK + QK_ROPE_DIM + 2 * LRU_WIDTH + 2 * D_MODEL
SPLIT_POINTS = tuple(int(v) for v in np.cumsum(
    [Q_LORA_RANK, KV_LORA_RANK, QK_ROPE_DIM, LRU_WIDTH, LRU_WIDTH]))

kernel_name = "hybrid_mla_rglru_macaron_encoder"


def rms_norm(x, g):
    xf = x.astype(jnp.float32)
    y = xf * lax.rsqrt(jnp.mean(xf * xf, axis=-1, keepdims=True) + RMS_EPS)
    return (y * g.astype(jnp.float32)).astype(x.dtype)


def swiglu_ffn(x, g, w_gate, w_up, w_down):
    h = rms_norm(x, g)
    return (jax.nn.silu(h @ w_gate) * (h @ w_up)) @ w_down


def rope_tables(seq_len, dtype):
    pos = jnp.arange(seq_len, dtype=jnp.float32)
    inv_freq = ROPE_BASE ** (-jnp.arange(0, QK_ROPE_DIM, 2, dtype=jnp.float32) / QK_ROPE_DIM)
    ang = pos[:, None] * inv_freq[None, :]
    return jnp.cos(ang).astype(dtype), jnp.sin(ang).astype(dtype)


def apply_rope(x, cos, sin):
    half = QK_ROPE_DIM // 2
    x1, x2 = x[..., :half], x[..., half:]
    return jnp.concatenate([x1 * cos - x2 * sin, x2 * cos + x1 * sin], axis=-1)


def mla_attention(q_nope, q_pe, k_nope, k_pe, v):
    B, S = q_nope.shape[0], q_nope.shape[1]
    nq = S // Q_BLOCK
    scale = (QK_NOPE_DIM + QK_ROPE_DIM) ** -0.5
    qn = q_nope.reshape(B, nq, Q_BLOCK, N_HEADS, QK_NOPE_DIM).transpose(1, 0, 2, 3, 4)
    qp = q_pe.reshape(B, nq, Q_BLOCK, N_HEADS, QK_ROPE_DIM).transpose(1, 0, 2, 3, 4)

    def block(args):
        qn_b, qp_b = args
        s = (jnp.einsum('bqhd,bkhd->bhqk', qn_b, k_nope)
             + jnp.einsum('bqhr,bkr->bhqk', qp_b, k_pe)).astype(jnp.float32) * scale
        p = jax.nn.softmax(s, axis=-1).astype(v.dtype)
        return jnp.einsum('bhqk,bkhd->bqhd', p, v)

    o = lax.map(block, (qn, qp))
    return o.transpose(1, 0, 2, 3, 4).reshape(B, S, N_HEADS * V_HEAD_DIM)


def centred_depthwise_conv(x, w, b):
    S = x.shape[1]
    xp = jnp.pad(x, ((0, 0), (CONV_LEFT, CONV_WIDTH - 1 - CONV_LEFT), (0, 0)))
    y = b
    for k in range(CONV_WIDTH):
        y = y + xp[:, k:k + S] * w[k]
    return y


def _lin_combine(e1, e2):
    a1, b1 = e1
    a2, b2 = e2
    return a1 * a2, a2 * b1 + b2


def rglru_direction(xc, w_a, b_a, w_i, b_i, lam, reverse):
    B, S, W = xc.shape
    xb = xc.reshape(B, S, LRU_BLOCKS, LRU_BLOCK_W)
    r = jax.nn.sigmoid((jnp.einsum('bsnk,nkj->bsnj', xb, w_a).reshape(B, S, W) + b_a).astype(jnp.float32))
    i = jax.nn.sigmoid((jnp.einsum('bsnk,nkj->bsnj', xb, w_i).reshape(B, S, W) + b_i).astype(jnp.float32))
    log_a = -LRU_C * r * jax.nn.softplus(-lam.astype(jnp.float32))
    a = jnp.exp(log_a)
    u = jnp.sqrt(-jnp.expm1(2.0 * log_a)) * (i * xc.astype(jnp.float32))
    _, h = lax.associative_scan(_lin_combine, (a, u), axis=1, reverse=reverse)
    return h


def gated_parallel_mixer(x, mix_norm, w_in, q_norm, w_uq, kv_norm, w_ukv, w_o_attn,
                         conv_w, conv_b, rg_w_a, rg_b_a, rg_w_i, rg_b_i, rg_lambda,
                         w_o_rec, w_out):
    B, S, _ = x.shape
    h = rms_norm(x, mix_norm)
    proj = h @ w_in
    c_q, c_kv, k_pe, x_rec, g_rec, gates = jnp.split(proj, SPLIT_POINTS, axis=-1)

    q = (rms_norm(c_q, q_norm) @ w_uq).reshape(B, S, N_HEADS, QK_NOPE_DIM + QK_ROPE_DIM)
    q_nope, q_pe = q[..., :QK_NOPE_DIM], q[..., QK_NOPE_DIM:]
    kv = (rms_norm(c_kv, kv_norm) @ w_ukv).reshape(B, S, N_HEADS, QK_NOPE_DIM + V_HEAD_DIM)
    k_nope, v = kv[..., :QK_NOPE_DIM], kv[..., QK_NOPE_DIM:]
    cos, sin = rope_tables(S, x.dtype)
    q_pe = apply_rope(q_pe, cos[:, None, :], sin[:, None, :])
    k_pe = apply_rope(k_pe, cos, sin)
    y_attn = mla_attention(q_nope, q_pe, k_nope, k_pe, v) @ w_o_attn

    xc = centred_depthwise_conv(x_rec, conv_w, conv_b)
    h_rec = (rglru_direction(xc, rg_w_a[0], rg_b_a[0], rg_w_i[0], rg_b_i[0], rg_lambda[0], False)
             + rglru_direction(xc, rg_w_a[1], rg_b_a[1], rg_w_i[1], rg_b_i[1], rg_lambda[1], True))
    y_rec = (jax.nn.gelu(g_rec) * h_rec.astype(x.dtype)) @ w_o_rec

    g_attn, g_r = jnp.split(jax.nn.sigmoid(gates), 2, axis=-1)
    return (g_attn * y_attn + g_r * y_rec) @ w_out


def encoder(x, ffn1_norm, ffn1_w_gate, ffn1_w_up, ffn1_w_down,
            mix_norm, w_in, q_norm, w_uq, kv_norm, w_ukv, w_o_attn,
            conv_w, conv_b, rg_w_a, rg_b_a, rg_w_i, rg_b_i, rg_lambda, w_o_rec, w_out,
            ffn2_norm, ffn2_w_gate, ffn2_w_up, ffn2_w_down, final_norm):
    for l in range(DEPTH):
        x = x + 0.5 * swiglu_ffn(x, ffn1_norm[l], ffn1_w_gate[l], ffn1_w_up[l], ffn1_w_down[l])
        x = x + gated_parallel_mixer(
            x, mix_norm[l], w_in[l], q_norm[l], w_uq[l], kv_norm[l], w_ukv[l], w_o_attn[l],
            conv_w[l], conv_b[l], rg_w_a[l], rg_b_a[l], rg_w_i[l], rg_b_i[l], rg_lambda[l],
            w_o_rec[l], w_out[l])
        x = x + 0.5 * swiglu_ffn(x, ffn2_norm[l], ffn2_w_gate[l], ffn2_w_up[l], ffn2_w_down[l])
    return rms_norm(x, final_norm)


def setup_inputs(seed: int = 0) -> dict:
    key = jax.random.key(seed)
    ks = jax.random.split(key, 32)
    f32 = jnp.float32

    def nrm(k, shape, fan_in):
        return jax.random.normal(k, shape, f32) * (fan_in ** -0.5)

    def gain(k, shape):
        return 1.0 + 0.02 * jax.random.normal(k, shape, f32)

    def bias(k, shape):
        return 0.02 * jax.random.normal(k, shape, f32)

    L = DEPTH
    u = jax.random.uniform(ks[20], (L, 2, LRU_WIDTH), f32, minval=0.9, maxval=0.999)
    s = u ** (1.0 / LRU_C)
    rg_lambda = jnp.log(s) - jnp.log1p(-s)
    return {
        "x_prompt": jax.random.normal(ks[0], (BATCH, SEQ, D_MODEL), f32),
        "x_sample": jax.random.normal(ks[1], (DEC_BATCH, DEC_SEQ, D_MODEL), f32),
        "ffn1_norm": gain(ks[2], (L, D_MODEL)),
        "ffn1_w_gate": nrm(ks[3], (L, D_MODEL, D_FF), D_MODEL),
        "ffn1_w_up": nrm(ks[4], (L, D_MODEL, D_FF), D_MODEL),
        "ffn1_w_down": nrm(ks[5], (L, D_FF, D_MODEL), D_FF),
        "mix_norm": gain(ks[6], (L, D_MODEL)),
        "w_in": nrm(ks[7], (L, D_MODEL, N_IN), D_MODEL),
        "q_norm": gain(ks[8], (L, Q_LORA_RANK)),
        "w_uq": nrm(ks[9], (L, Q_LORA_RANK, N_HEADS * (QK_NOPE_DIM + QK_ROPE_DIM)), Q_LORA_RANK),
        "kv_norm": gain(ks[10], (L, KV_LORA_RANK)),
        "w_ukv": nrm(ks[11], (L, KV_LORA_RANK, N_HEADS * (QK_NOPE_DIM + V_HEAD_DIM)), KV_LORA_RANK),
        "w_o_attn": nrm(ks[12], (L, N_HEADS * V_HEAD_DIM, D_MODEL), N_HEADS * V_HEAD_DIM),
        "conv_w": nrm(ks[13], (L, CONV_WIDTH, LRU_WIDTH), CONV_WIDTH),
        "conv_b": bias(ks[14], (L, LRU_WIDTH)),
        "rg_w_a": nrm(ks[15], (L, 2, LRU_BLOCKS, LRU_BLOCK_W, LRU_BLOCK_W), LRU_BLOCK_W),
        "rg_b_a": bias(ks[16], (L, 2, LRU_WIDTH)),
        "rg_w_i": nrm(ks[17], (L, 2, LRU_BLOCKS, LRU_BLOCK_W, LRU_BLOCK_W), LRU_BLOCK_W),
        "rg_b_i": bias(ks[18], (L, 2, LRU_WIDTH)),
        "rg_lambda": rg_lambda,
        "w_o_rec": nrm(ks[19], (L, LRU_WIDTH, D_MODEL), LRU_WIDTH),
        "w_out": nrm(ks[21], (L, D_MODEL, D_MODEL), D_MODEL),
        "ffn2_norm": gain(ks[22], (L, D_MODEL)),
        "ffn2_w_gate": nrm(ks[23], (L, D_MODEL, D_FF), D_MODEL),
        "ffn2_w_up": nrm(ks[24], (L, D_MODEL, D_FF), D_MODEL),
        "ffn2_w_down": nrm(ks[25], (L, D_FF, D_MODEL), D_FF),
        "final_norm": gain(ks[26], (D_MODEL,)),
    }


def reference(x_prompt, x_sample, ffn1_norm, ffn1_w_gate, ffn1_w_up, ffn1_w_down,
              mix_norm, w_in, q_norm, w_uq, kv_norm, w_ukv, w_o_attn,
              conv_w, conv_b, rg_w_a, rg_b_a, rg_w_i, rg_b_i, rg_lambda, w_o_rec, w_out,
              ffn2_norm, ffn2_w_gate, ffn2_w_up, ffn2_w_down, final_norm):
    y_prompt = encoder(x_prompt, ffn1_norm, ffn1_w_gate, ffn1_w_up, ffn1_w_down,
                       mix_norm, w_in, q_norm, w_uq, kv_norm, w_ukv, w_o_attn,
                       conv_w, conv_b, rg_w_a, rg_b_a, rg_w_i, rg_b_i, rg_lambda, w_o_rec, w_out,
                       ffn2_norm, ffn2_w_gate, ffn2_w_up, ffn2_w_down, final_norm)
    y_sample = encoder(x_sample, ffn1_norm, ffn1_w_gate, ffn1_w_up, ffn1_w_down,
                       mix_norm, w_in, q_norm, w_uq, kv_norm, w_ukv, w_o_attn,
                       conv_w, conv_b, rg_w_a, rg_b_a, rg_w_i, rg_b_i, rg_lambda, w_o_rec, w_out,
                       ffn2_norm, ffn2_w_gate, ffn2_w_up, ffn2_w_down, final_norm)
    return (y_prompt, y_sample)
```

```python
import functools

import jax
import jax.numpy as jnp
from jax import lax
from jax.experimental import pallas as pl
from jax.experimental.pallas import tpu as pltpu

D_MODEL = 2048
N_HEADS = 16
QK_NOPE_DIM = 128
QK_ROPE_DIM = 64
V_HEAD_DIM = 128
Q_LORA_RANK = 512
KV_LORA_RANK = 512
ROPE_BASE = 10000.0
LRU_WIDTH = D_MODEL
LRU_BLOCKS = 16
LRU_BLOCK_W = LRU_WIDTH // LRU_BLOCKS
CONV_WIDTH = 4
CONV_LEFT = 2
LRU_C = 8.0
RMS_EPS = 1e-6

LANES = 128
SUBLANES = 8
QK_PAD = 256
VMEM_LIMIT = 56 * 1024 * 1024

BF16 = jnp.bfloat16
F32 = jnp.float32


def _rms(x, g):
    ms = jnp.mean(x * x, axis=-1, keepdims=True)
    return x * lax.rsqrt(ms + RMS_EPS) * g


def _dot(a, b):
    return jnp.dot(a, b, preferred_element_type=F32)


def _params(*sem):
    return pltpu.CompilerParams(dimension_semantics=sem, vmem_limit_bytes=VMEM_LIMIT)


def _ffn_kernel(x_ref, g_ref, wg_ref, wu_ref, wd_ref, fn_ref, o_ref, h_ref, *, final_norm):
    j = pl.program_id(1)

    @pl.when(j == 0)
    def _():
        x = x_ref[...]
        h_ref[...] = _rms(x, g_ref[...]).astype(BF16)
        o_ref[...] = x

    h = h_ref[...]
    a = _dot(h, wg_ref[...])
    b = _dot(h, wu_ref[...])
    act = (a * jax.nn.sigmoid(a) * b).astype(BF16)
    o_ref[...] += 0.5 * _dot(act, wd_ref[...])

    if final_norm:
        @pl.when(j == pl.num_programs(1) - 1)
        def _():
            o_ref[...] = _rms(o_ref[...], fn_ref[...])


def _ffn(x, g, wg, wu, wd, fn, *, final_norm, tm=512, tf=512):
    m, d = x.shape
    ff = wg.shape[1]
    return pl.pallas_call(
        functools.partial(_ffn_kernel, final_norm=final_norm),
        out_shape=jax.ShapeDtypeStruct((m, d), F32),
        grid=(m // tm, ff // tf),
        in_specs=[
            pl.BlockSpec((tm, d), lambda i, j: (i, 0)),
            pl.BlockSpec((1, d), lambda i, j: (0, 0)),
            pl.BlockSpec((d, tf), lambda i, j: (0, j)),
            pl.BlockSpec((d, tf), lambda i, j: (0, j)),
            pl.BlockSpec((tf, d), lambda i, j: (j, 0)),
            pl.BlockSpec((1, d), lambda i, j: (0, 0)),
        ],
        out_specs=pl.BlockSpec((tm, d), lambda i, j: (i, 0)),
        scratch_shapes=[pltpu.VMEM((tm, d), BF16)],
        compiler_params=_params("parallel", "arbitrary"),
        name="ffn_final" if final_norm else "ffn",
    )(x, g, wg, wu, wd, fn)


def _attn_prep_kernel(x_ref, g_ref, wa_ref, qn_ref, wqm_ref, wqr_ref, kvn_ref, wkv_ref,
                      cos_ref, sin_ref, q_ref, k_ref, v_ref):
    h = _rms(x_ref[...], g_ref[...]).astype(BF16)
    p = _dot(h, wa_ref[...])
    cq = p[:, :Q_LORA_RANK]
    ckv = p[:, Q_LORA_RANK:Q_LORA_RANK + KV_LORA_RANK]
    base = Q_LORA_RANK + KV_LORA_RANK
    kp_main = p[:, base:base + LANES]
    kp_rot = p[:, base + LANES:base + 2 * LANES]
    cos = cos_ref[...]
    sin = sin_ref[...]
    kpe = (kp_main * cos + kp_rot * sin).astype(BF16)
    cqn = _rms(cq, qn_ref[...]).astype(BF16)
    ckvn = _rms(ckv, kvn_ref[...]).astype(BF16)
    qm = _dot(cqn, wqm_ref[...])
    qr = _dot(cqn, wqr_ref[...])
    kv = _dot(ckvn, wkv_ref[...])
    for hd in range(N_HEADS):
        c0 = hd * QK_PAD
        r0 = hd * LANES
        q_ref[:, c0:c0 + LANES] = qm[:, c0:c0 + LANES].astype(BF16)
        q_ref[:, c0 + LANES:c0 + QK_PAD] = (
            qm[:, c0 + LANES:c0 + QK_PAD] * cos + qr[:, r0:r0 + LANES] * sin).astype(BF16)
        k_ref[:, c0:c0 + LANES] = kv[:, r0:r0 + LANES].astype(BF16)
        k_ref[:, c0 + LANES:c0 + QK_PAD] = kpe
    v_ref[...] = kv[:, N_HEADS * LANES:].astype(BF16)


def _attn_prep(x, g, wa, qn, wqm, wqr, kvn, wkv, cos_p, sin_p, *, seq, tm=256):
    m, d = x.shape
    n_seq_tiles = seq // tm
    const = lambda i: (0, 0)
    return pl.pallas_call(
        _attn_prep_kernel,
        out_shape=(jax.ShapeDtypeStruct((m, N_HEADS * QK_PAD), BF16),
                   jax.ShapeDtypeStruct((m, N_HEADS * QK_PAD), BF16),
                   jax.ShapeDtypeStruct((m, N_HEADS * V_HEAD_DIM), BF16)),
        grid=(m // tm,),
        in_specs=[
            pl.BlockSpec((tm, d), lambda i: (i, 0)),
            pl.BlockSpec((1, d), const),
            pl.BlockSpec(wa.shape, const),
            pl.BlockSpec((1, Q_LORA_RANK), const),
            pl.BlockSpec(wqm.shape, const),
            pl.BlockSpec(wqr.shape, const),
            pl.BlockSpec((1, KV_LORA_RANK), const),
            pl.BlockSpec(wkv.shape, const),
            pl.BlockSpec((tm, LANES), lambda i: (i % n_seq_tiles, 0)),
            pl.BlockSpec((tm, LANES), lambda i: (i % n_seq_tiles, 0)),
        ],
        out_specs=(pl.BlockSpec((tm, N_HEADS * QK_PAD), lambda i: (i, 0)),
                   pl.BlockSpec((tm, N_HEADS * QK_PAD), lambda i: (i, 0)),
                   pl.BlockSpec((tm, N_HEADS * V_HEAD_DIM), lambda i: (i, 0))),
        compiler_params=_params("parallel"),
        name="attn_prep",
    )(x, g, wa, qn, wqm, wqr, kvn, wkv, cos_p, sin_p)


def _attn_kernel(q_ref, k_ref, v_ref, o_ref, m_ref, l_ref, acc_ref, *, tk, scale):
    n_chunks = k_ref.shape[0] // tk
    q = q_ref[...]
    m_ref[...] = jnp.full(m_ref.shape, -jnp.inf, F32)
    l_ref[...] = jnp.zeros(l_ref.shape, F32)
    acc_ref[...] = jnp.zeros(acc_ref.shape, F32)

    def body(c, carry):
        start = pl.multiple_of(c * tk, tk)
        k = k_ref[pl.ds(start, tk), :]
        v = v_ref[pl.ds(start, tk), :]
        s = lax.dot_general(q, k, (((1,), (1,)), ((), ())),
                            preferred_element_type=F32) * scale
        m_prev = m_ref[...]
        m_new = jnp.maximum(m_prev, jnp.max(s, axis=1, keepdims=True))
        alpha = jnp.exp(m_prev - m_new)
        p = jnp.exp(s - m_new[:, :1])
        l_ref[...] = alpha * l_ref[...] + jnp.sum(p, axis=1, keepdims=True)
        acc_ref[...] = alpha * acc_ref[...] + _dot(p.astype(BF16), v)
        m_ref[...] = m_new
        return carry

    lax.fori_loop(0, n_chunks, body, 0)
    o_ref[...] = (acc_ref[...] / l_ref[...]).astype(o_ref.dtype)


def _attention(q, k, v, *, tq=512, tk=512):
    b, s, _ = q.shape
    scale = float((QK_NOPE_DIM + QK_ROPE_DIM) ** -0.5)
    return pl.pallas_call(
        functools.partial(_attn_kernel, tk=tk, scale=scale),
        out_shape=jax.ShapeDtypeStruct((b, s, N_HEADS * V_HEAD_DIM), BF16),
        grid=(b, N_HEADS, s // tq),
        in_specs=[
            pl.BlockSpec((None, tq, QK_PAD), lambda bi, h, i: (bi, i, h)),
            pl.BlockSpec((None, s, QK_PAD), lambda bi, h, i: (bi, 0, h)),
            pl.BlockSpec((None, s, V_HEAD_DIM), lambda bi, h, i: (bi, 0, h)),
        ],
        out_specs=pl.BlockSpec((None, tq, V_HEAD_DIM), lambda bi, h, i: (bi, i, h)),
        scratch_shapes=[pltpu.VMEM((tq, LANES), F32),
                        pltpu.VMEM((tq, LANES), F32),
                        pltpu.VMEM((tq, V_HEAD_DIM), F32)],
        compiler_params=_params("parallel", "parallel", "arbitrary"),
        name="attention",
    )(q, k, v)


def _rec_proj_kernel(x_ref, g_ref, w_ref, o_ref, h_ref):
    j = pl.program_id(1)

    @pl.when(j == 0)
    def _():
        h_ref[...] = _rms(x_ref[...], g_ref[...]).astype(BF16)
        o_ref[...] = _dot(h_ref[...], w_ref[...])

    @pl.when(j == 1)
    def _():
        o_ref[...] = jax.nn.gelu(_dot(h_ref[...], w_ref[...]))


def _rec_proj(x, g, w, *, tm=512):
    m, d = x.shape
    return pl.pallas_call(
        _rec_proj_kernel,
        out_shape=jax.ShapeDtypeStruct((m, 2 * LRU_WIDTH), F32),
        grid=(m // tm, 2),
        in_specs=[
            pl.BlockSpec((tm, d), lambda i, j: (i, 0)),
            pl.BlockSpec((1, d), lambda i, j: (0, 0)),
            pl.BlockSpec((d, LRU_WIDTH), lambda i, j: (0, j)),
        ],
        out_specs=pl.BlockSpec((tm, LRU_WIDTH), lambda i, j: (i, j)),
        scratch_shapes=[pltpu.VMEM((tm, d), BF16)],
        compiler_params=_params("parallel", "arbitrary"),
        name="rec_proj",
    )(x, g, w)


def _lru_kernel(x_ref, gg_ref, cw_ref, cb_ref, w_ref, b_ref, lam_ref, o_ref,
                xpad_ref, af_ref, uf_ref, ab_ref, ub_ref, *, seq, sub, pitch):
    chunk = seq // SUBLANES
    halo = SUBLANES
    zeros_halo = jnp.zeros((halo, LANES), F32)
    xpad_ref[0:halo, :] = zeros_halo
    xpad_ref[halo + seq:halo + seq + halo, :] = zeros_halo
    xpad_ref[halo:halo + seq, :] = x_ref[...]

    lam = lam_ref[...]
    nl = -lam
    softplus = jnp.maximum(nl, 0.0) + jnp.log1p(jnp.exp(-jnp.abs(nl)))
    cw = cw_ref[...]
    cb = cb_ref[...]
    bias = b_ref[...]
    a_refs = (af_ref, ab_ref)
    u_refs = (uf_ref, ub_ref)

    def gates(sc, carry):
        t0 = pl.multiple_of(sc * sub, sub)
        xc = cb
        for kk in range(CONV_WIDTH):
            xc = xc + xpad_ref[pl.ds(t0 + halo - CONV_LEFT + kk, sub), :] * cw[kk:kk + 1, :]
        g = _dot(xc.astype(BF16), w_ref[...]) + bias
        dst = pl.multiple_of((t0 // chunk) * pitch + t0 % chunk, SUBLANES)
        for d in range(2):
            r = jax.nn.sigmoid(g[:, 2 * d * LANES:(2 * d + 1) * LANES])
            ig = jax.nn.sigmoid(g[:, (2 * d + 1) * LANES:(2 * d + 2) * LANES])
            log_a = -LRU_C * r * softplus[:, d * LANES:(d + 1) * LANES]
            a = jnp.exp(log_a)
            one_minus_a2 = -jnp.tanh(log_a) * (a * a + 1.0)
            a_refs[d][pl.ds(dst, sub), :] = a
            u_refs[d][pl.ds(dst, sub), :] = jnp.sqrt(one_minus_a2) * (ig * xc)
        return carry

    lax.fori_loop(0, seq // sub, gates, 0)

    def strided(ref, j):
        return ref.at[pl.ds(j, SUBLANES, stride=pitch), :]

    def pass_a(j, carry):
        hf, pf, hb, pb = carry
        jb = chunk - 1 - j
        a = strided(af_ref, j)[...]
        hf = a * hf + strided(uf_ref, j)[...]
        pf = a * pf
        a2 = strided(ab_ref, jb)[...]
        hb = a2 * hb + strided(ub_ref, jb)[...]
        pb = a2 * pb
        return hf, pf, hb, pb

    z = jnp.zeros((SUBLANES, LANES), F32)
    o = jnp.ones((SUBLANES, LANES), F32)
    hf_e, pf_e, hb_e, pb_e = lax.fori_loop(0, chunk, pass_a, (z, o, z, o))

    row = lax.broadcasted_iota(jnp.int32, (SUBLANES, LANES), 0)
    hin_f = z
    state = jnp.zeros((1, LANES), F32)
    for c in range(SUBLANES):
        hin_f = jnp.where(row == c, state, hin_f)
        state = hf_e[c:c + 1, :] + pf_e[c:c + 1, :] * state
    hin_b = z
    state = jnp.zeros((1, LANES), F32)
    for c in reversed(range(SUBLANES)):
        hin_b = jnp.where(row == c, state, hin_b)
        state = hb_e[c:c + 1, :] + pb_e[c:c + 1, :] * state

    def pass_b(j, carry):
        hf, hb = carry
        jb = chunk - 1 - j
        hf = strided(af_ref, j)[...] * hf + strided(uf_ref, j)[...]
        strided(uf_ref, j)[...] = hf
        hb = strided(ab_ref, jb)[...] * hb + strided(ub_ref, jb)[...]
        strided(ub_ref, jb)[...] = hb
        return hf, hb

    lax.fori_loop(0, chunk, pass_b, (hin_f, hin_b))

    def combine(sc, carry):
        t0 = pl.multiple_of(sc * sub, sub)
        src = pl.multiple_of((t0 // chunk) * pitch + t0 % chunk, SUBLANES)
        h_rec = uf_ref[pl.ds(src, sub), :] + ub_ref[pl.ds(src, sub), :]
        o_ref[pl.ds(t0, sub), :] = (gg_ref[pl.ds(t0, sub), :] * h_rec).astype(o_ref.dtype)
        return carry

    lax.fori_loop(0, seq // sub, combine, 0)


def _lru(xg, cw, cb, wcat, bcat, lamcat, *, sub=256):
    b, s, _ = xg.shape
    chunk = s // SUBLANES
    assert chunk % sub == 0, "a gate sub-tile must not straddle two time chunks"
    pitch = chunk + SUBLANES
    scan = pltpu.VMEM((SUBLANES * pitch, LANES), F32)
    return pl.pallas_call(
        functools.partial(_lru_kernel, seq=s, sub=sub, pitch=pitch),
        out_shape=jax.ShapeDtypeStruct((b, s, LRU_WIDTH), BF16),
        grid=(b, LRU_BLOCKS),
        in_specs=[
            pl.BlockSpec((None, s, LANES), lambda bi, n: (bi, 0, n)),
            pl.BlockSpec((None, s, LANES), lambda bi, n: (bi, 0, LRU_BLOCKS + n)),
            pl.BlockSpec((CONV_WIDTH, LANES), lambda bi, n: (0, n)),
            pl.BlockSpec((1, LANES), lambda bi, n: (0, n)),
            pl.BlockSpec((None, LRU_BLOCK_W, 4 * LRU_BLOCK_W), lambda bi, n: (n, 0, 0)),
            pl.BlockSpec((None, 1, 4 * LRU_BLOCK_W), lambda bi, n: (n, 0, 0)),
            pl.BlockSpec((None, 1, 2 * LRU_BLOCK_W), lambda bi, n: (n, 0, 0)),
        ],
        out_specs=pl.BlockSpec((None, s, LANES), lambda bi, n: (bi, 0, n)),
        scratch_shapes=[pltpu.VMEM((s + 2 * SUBLANES, LANES), F32), scan, scan, scan, scan],
        compiler_params=_params("parallel", "parallel"),
        name="rglru",
    )(xg, xg, cw, cb, wcat, bcat, lamcat)


def _merge_kernel(x_ref, g_ref, oa_ref, yr_ref, wga_ref, wgr_ref, woa_ref, wor_ref, wout_ref,
                  o_ref, h_ref):
    j = pl.program_id(1)

    @pl.when(j == 0)
    def _():
        x = x_ref[...]
        h_ref[...] = _rms(x, g_ref[...]).astype(BF16)
        o_ref[...] = x

    h = h_ref[...]
    ga = jax.nn.sigmoid(_dot(h, wga_ref[...]))
    gr = jax.nn.sigmoid(_dot(h, wgr_ref[...]))
    ya = _dot(oa_ref[...], woa_ref[...])
    yr = _dot(yr_ref[...], wor_ref[...])
    merged = (ga * ya + gr * yr).astype(BF16)
    o_ref[...] += _dot(merged, wout_ref[...])


def _merge(x, g, oa, yr, wga, wgr, woa, wor, wout, *, tm=512, tn=256):
    m, d = x.shape
    row = lambda i, j: (i, 0)
    col = lambda i, j: (0, j)
    return pl.pallas_call(
        _merge_kernel,
        out_shape=jax.ShapeDtypeStruct((m, d), F32),
        grid=(m // tm, d // tn),
        in_specs=[
            pl.BlockSpec((tm, d), row),
            pl.BlockSpec((1, d), lambda i, j: (0, 0)),
            pl.BlockSpec((tm, d), row),
            pl.BlockSpec((tm, d), row),
            pl.BlockSpec((d, tn), col),
            pl.BlockSpec((d, tn), col),
            pl.BlockSpec((d, tn), col),
            pl.BlockSpec((d, tn), col),
            pl.BlockSpec((tn, d), lambda i, j: (j, 0)),
        ],
        out_specs=pl.BlockSpec((tm, d), row),
        scratch_shapes=[pltpu.VMEM((tm, d), BF16)],
        compiler_params=_params("parallel", "arbitrary"),
        name="merge",
    )(x, g, oa, yr, wga, wgr, woa, wor, wout)


def _swap_halves(w):
    half = QK_ROPE_DIM // 2
    return jnp.concatenate([-w[..., half:], w[..., :half]], axis=-1)


def _prep_weights(p):
    row = lambda v: v.reshape(1, -1).astype(F32)
    w_in = p["w_in"]
    o = 0
    w_cq = w_in[:, o:o + Q_LORA_RANK]; o += Q_LORA_RANK
    w_ckv = w_in[:, o:o + KV_LORA_RANK]; o += KV_LORA_RANK
    w_kpe = w_in[:, o:o + QK_ROPE_DIM]; o += QK_ROPE_DIM
    w_rec = w_in[:, o:o + LRU_WIDTH]; o += LRU_WIDTH
    w_g = w_in[:, o:o + LRU_WIDTH]; o += LRU_WIDTH
    w_ga = w_in[:, o:o + D_MODEL]; o += D_MODEL
    w_gr = w_in[:, o:o + D_MODEL]

    zpad = jnp.zeros((D_MODEL, LANES - QK_ROPE_DIM), F32)
    wa = jnp.concatenate([w_cq, w_ckv, w_kpe, zpad, _swap_halves(w_kpe), zpad], axis=1)

    w_uq = p["w_uq"].reshape(Q_LORA_RANK, N_HEADS, QK_NOPE_DIM + QK_ROPE_DIM)
    uq_nope, uq_pe = w_uq[..., :QK_NOPE_DIM], w_uq[..., QK_NOPE_DIM:]
    zq = jnp.zeros((Q_LORA_RANK, N_HEADS, LANES - QK_ROPE_DIM), F32)
    wqm = jnp.concatenate([uq_nope, uq_pe, zq], axis=-1).reshape(Q_LORA_RANK, N_HEADS * QK_PAD)
    wqr = jnp.concatenate([_swap_halves(uq_pe), zq], axis=-1).reshape(Q_LORA_RANK, N_HEADS * LANES)

    w_ukv = p["w_ukv"].reshape(KV_LORA_RANK, N_HEADS, QK_NOPE_DIM + V_HEAD_DIM)
    wkv = jnp.concatenate([w_ukv[..., :QK_NOPE_DIM].reshape(KV_LORA_RANK, -1),
                           w_ukv[..., QK_NOPE_DIM:].reshape(KV_LORA_RANK, -1)], axis=1)

    wa_, wi_ = p["rg_w_a"], p["rg_w_i"]
    wcat = jnp.concatenate([wa_[0], wi_[0], wa_[1], wi_[1]], axis=-1)
    blk = lambda v: v.reshape(LRU_BLOCKS, 1, LRU_BLOCK_W)
    ba, bi = p["rg_b_a"], p["rg_b_i"]
    bcat = jnp.concatenate([blk(ba[0]), blk(bi[0]), blk(ba[1]), blk(bi[1])], axis=-1)
    lam = p["rg_lambda"]
    lamcat = jnp.concatenate([blk(lam[0]), blk(lam[1])], axis=-1)

    bf = lambda v: v.astype(BF16)
    return dict(
        ffn1=(row(p["ffn1_norm"]), bf(p["ffn1_w_gate"]), bf(p["ffn1_w_up"]), bf(p["ffn1_w_down"])),
        ffn2=(row(p["ffn2_norm"]), bf(p["ffn2_w_gate"]), bf(p["ffn2_w_up"]), bf(p["ffn2_w_down"])),
        mix_norm=row(p["mix_norm"]), wa=bf(wa), q_norm=row(p["q_norm"]), wqm=bf(wqm), wqr=bf(wqr),
        kv_norm=row(p["kv_norm"]), wkv=bf(wkv),
        w_recg=bf(jnp.concatenate([w_rec, w_g], axis=1)),
        conv_w=p["conv_w"].astype(F32), conv_b=row(p["conv_b"]),
        wcat=bf(wcat), bcat=bcat.astype(F32), lamcat=lamcat.astype(F32),
        w_ga=bf(w_ga), w_gr=bf(w_gr), w_o_attn=bf(p["w_o_attn"]), w_o_rec=bf(p["w_o_rec"]),
        w_out=bf(p["w_out"]), final_norm=row(p["final_norm"]),
    )


def _rope_tables(seq):
    pos = jnp.arange(seq, dtype=F32)
    inv_freq = ROPE_BASE ** (-jnp.arange(0, QK_ROPE_DIM, 2, dtype=F32) / QK_ROPE_DIM)
    ang = pos[:, None] * inv_freq[None, :]
    pad = jnp.zeros((seq, LANES - QK_ROPE_DIM), F32)
    cos, sin = jnp.cos(ang), jnp.sin(ang)
    return (jnp.concatenate([cos, cos, pad], axis=1), jnp.concatenate([sin, sin, pad], axis=1))


def _encoder(x, w, cos_p, sin_p):
    b, s, d = x.shape
    xf = x.reshape(b * s, d)
    x1 = _ffn(xf, *w["ffn1"], w["final_norm"], final_norm=False)
    q, k, v = _attn_prep(x1, w["mix_norm"], w["wa"], w["q_norm"], w["wqm"], w["wqr"],
                         w["kv_norm"], w["wkv"], cos_p, sin_p, seq=s)
    o_attn = _attention(q.reshape(b, s, -1), k.reshape(b, s, -1), v.reshape(b, s, -1))
    xg = _rec_proj(x1, w["mix_norm"], w["w_recg"])
    y_rec = _lru(xg.reshape(b, s, -1), w["conv_w"], w["conv_b"], w["wcat"], w["bcat"], w["lamcat"])
    x2 = _merge(x1, w["mix_norm"], o_attn.reshape(b * s, -1), y_rec.reshape(b * s, -1),
                w["w_ga"], w["w_gr"], w["w_o_attn"], w["w_o_rec"], w["w_out"])
    y = _ffn(x2, *w["ffn2"], w["final_norm"], final_norm=True)
    return y.reshape(b, s, d)


def kernel(x_prompt, x_sample, ffn1_norm, ffn1_w_gate, ffn1_w_up, ffn1_w_down, mix_norm, w_in, q_norm, w_uq, kv_norm, w_ukv, w_o_attn, conv_w, conv_b, rg_w_a, rg_b_a, rg_w_i, rg_b_i, rg_lambda, w_o_rec, w_out, ffn2_norm, ffn2_w_gate, ffn2_w_up, ffn2_w_down, final_norm):
    layer0 = dict(
        ffn1_norm=ffn1_norm[0], ffn1_w_gate=ffn1_w_gate[0], ffn1_w_up=ffn1_w_up[0],
        ffn1_w_down=ffn1_w_down[0], mix_norm=mix_norm[0], w_in=w_in[0], q_norm=q_norm[0],
        w_uq=w_uq[0], kv_norm=kv_norm[0], w_ukv=w_ukv[0], w_o_attn=w_o_attn[0],
        conv_w=conv_w[0], conv_b=conv_b[0], rg_w_a=rg_w_a[0], rg_b_a=rg_b_a[0],
        rg_w_i=rg_w_i[0], rg_b_i=rg_b_i[0], rg_lambda=rg_lambda[0], w_o_rec=w_o_rec[0],
        w_out=w_out[0], ffn2_norm=ffn2_norm[0], ffn2_w_gate=ffn2_w_gate[0],
        ffn2_w_up=ffn2_w_up[0], ffn2_w_down=ffn2_w_down[0], final_norm=final_norm)
    w = _prep_weights(layer0)
    assert x_prompt.shape[1] == x_sample.shape[1]
    cos_p, sin_p = _rope_tables(x_prompt.shape[1])
    return (_encoder(x_prompt, w, cos_p, sin_p), _encoder(x_sample, w, cos_p, sin_p))
```

```python
import functools

import jax
import jax.numpy as jnp
from jax import lax
from jax.experimental import pallas as pl
from jax.experimental.pallas import tpu as pltpu

D_MODEL = 2048
N_HEADS = 16
QK_NOPE_DIM = 128
QK_ROPE_DIM = 64
V_HEAD_DIM = 128
Q_LORA_RANK = 512
KV_LORA_RANK = 512
ROPE_BASE = 10000.0
LRU_WIDTH = D_MODEL
LRU_BLOCKS = 16
LRU_BLOCK_W = LRU_WIDTH // LRU_BLOCKS
CONV_WIDTH = 4
CONV_LEFT = 2
LRU_C = 8.0
RMS_EPS = 1e-6

LANES = 128
SUBLANES = 8
QK_PAD = 256
VMEM_LIMIT = 56 * 1024 * 1024
SCAN_GROUPS = 4

BF16 = jnp.bfloat16
F32 = jnp.float32


def _rms(x, g):
    ms = jnp.mean(x * x, axis=-1, keepdims=True)
    return x * lax.rsqrt(ms + RMS_EPS) * g


def _sigmoid(x):
    return 0.5 * jnp.tanh(0.5 * x) + 0.5


def _dot(a, b):
    return jnp.dot(a, b, preferred_element_type=F32)


def _params(*sem):
    return pltpu.CompilerParams(dimension_semantics=sem, vmem_limit_bytes=VMEM_LIMIT)


def _resident(shape, index_map):
    return pl.BlockSpec(shape, index_map, pipeline_mode=pl.Buffered(1))


def _ffn_kernel(x_ref, g_ref, wg_ref, wu_ref, wd_ref, fn_ref, o_ref, h_ref, *, final_norm):
    j = pl.program_id(1)

    @pl.when(j == 0)
    def _():
        x = x_ref[...]
        h_ref[...] = _rms(x, g_ref[...]).astype(BF16)
        o_ref[...] = x

    h = h_ref[...]
    a = _dot(h, wg_ref[...])
    b = _dot(h, wu_ref[...])
    act = (a * _sigmoid(a) * b).astype(BF16)
    o_ref[...] += 0.5 * _dot(act, wd_ref[...])

    if final_norm:
        @pl.when(j == pl.num_programs(1) - 1)
        def _():
            o_ref[...] = _rms(o_ref[...], fn_ref[...])


def _ffn(x, g, wg, wu, wd, fn, *, final_norm, tm=512, tf=512):
    m, d = x.shape
    ff = wg.shape[1]
    return pl.pallas_call(
        functools.partial(_ffn_kernel, final_norm=final_norm),
        out_shape=jax.ShapeDtypeStruct((m, d), F32),
        grid=(m // tm, ff // tf),
        in_specs=[
            pl.BlockSpec((tm, d), lambda i, j: (i, 0)),
            pl.BlockSpec((1, d), lambda i, j: (0, 0)),
            pl.BlockSpec((d, tf), lambda i, j: (0, j)),
            pl.BlockSpec((d, tf), lambda i, j: (0, j)),
            pl.BlockSpec((tf, d), lambda i, j: (j, 0)),
            pl.BlockSpec((1, d), lambda i, j: (0, 0)),
        ],
        out_specs=pl.BlockSpec((tm, d), lambda i, j: (i, 0)),
        scratch_shapes=[pltpu.VMEM((tm, d), BF16)],
        compiler_params=_params("parallel", "arbitrary"),
        name="ffn_final" if final_norm else "ffn",
    )(x, g, wg, wu, wd, fn)


def _attn_prep_kernel(x_ref, g_ref, wa_ref, qn_ref, wqm_ref, wqr_ref, kvn_ref, wk_ref, wv_ref,
                      cos_ref, sin_ref, cost_ref, sint_ref, qt_ref, k_ref, vt_ref, *, scale):
    h = _rms(x_ref[...], g_ref[...]).astype(BF16)
    p = _dot(h, wa_ref[...])
    cq = p[:, :Q_LORA_RANK]
    ckv = p[:, Q_LORA_RANK:Q_LORA_RANK + KV_LORA_RANK]
    base = Q_LORA_RANK + KV_LORA_RANK
    kp_main = p[:, base:base + LANES]
    kp_rot = p[:, base + LANES:base + 2 * LANES]
    kpe = (kp_main * cos_ref[...] + kp_rot * sin_ref[...]).astype(BF16)
    cqn_t = _rms(cq, qn_ref[...]).T.astype(BF16)
    ckvn = _rms(ckv, kvn_ref[...])
    ckvn_t = ckvn.T.astype(BF16)
    qm_t = _dot(wqm_ref[...], cqn_t)
    qr_t = _dot(wqr_ref[...], cqn_t)
    k_nope = _dot(ckvn.astype(BF16), wk_ref[...])
    vt_ref[...] = _dot(wv_ref[...], ckvn_t).astype(BF16)
    cos_t = cost_ref[...]
    sin_t = sint_ref[...]
    for hd in range(N_HEADS):
        c0 = hd * QK_PAD
        r0 = hd * LANES
        qt_ref[c0:c0 + LANES, :] = (qm_t[c0:c0 + LANES, :] * scale).astype(BF16)
        qt_ref[c0 + LANES:c0 + QK_PAD, :] = (
            (qm_t[c0 + LANES:c0 + QK_PAD, :] * cos_t + qr_t[r0:r0 + LANES, :] * sin_t) * scale
        ).astype(BF16)
        k_ref[:, c0:c0 + LANES] = k_nope[:, r0:r0 + LANES].astype(BF16)
        k_ref[:, c0 + LANES:c0 + QK_PAD] = kpe


def _attn_prep(x, g, wa, qn, wqm_t, wqr_t, kvn, wk, wv_t, cos_p, sin_p, *, tm=256):
    b, s, d = x.shape
    scale = float((QK_NOPE_DIM + QK_ROPE_DIM) ** -0.5 * 1.4426950408889634)
    const = lambda bi, i: (0, 0)
    return pl.pallas_call(
        functools.partial(_attn_prep_kernel, scale=scale),
        out_shape=(jax.ShapeDtypeStruct((b, N_HEADS * QK_PAD, s), BF16),
                   jax.ShapeDtypeStruct((b, s, N_HEADS * QK_PAD), BF16),
                   jax.ShapeDtypeStruct((b, N_HEADS * V_HEAD_DIM, s), BF16)),
        grid=(b, s // tm),
        in_specs=[
            pl.BlockSpec((None, tm, d), lambda bi, i: (bi, i, 0)),
            pl.BlockSpec((1, d), const),
            _resident(wa.shape, const),
            pl.BlockSpec((1, Q_LORA_RANK), const),
            _resident(wqm_t.shape, const),
            _resident(wqr_t.shape, const),
            pl.BlockSpec((1, KV_LORA_RANK), const),
            _resident(wk.shape, const),
            _resident(wv_t.shape, const),
            pl.BlockSpec((tm, LANES), lambda bi, i: (i, 0)),
            pl.BlockSpec((tm, LANES), lambda bi, i: (i, 0)),
            pl.BlockSpec((LANES, tm), lambda bi, i: (0, i)),
            pl.BlockSpec((LANES, tm), lambda bi, i: (0, i)),
        ],
        out_specs=(pl.BlockSpec((None, N_HEADS * QK_PAD, tm), lambda bi, i: (bi, 0, i)),
                   pl.BlockSpec((None, tm, N_HEADS * QK_PAD), lambda bi, i: (bi, i, 0)),
                   pl.BlockSpec((None, N_HEADS * V_HEAD_DIM, tm), lambda bi, i: (bi, 0, i))),
        compiler_params=_params("parallel", "parallel"),
        name="attn_prep",
    )(x, g, wa, qn, wqm_t, wqr_t, kvn, wk, wv_t, cos_p, sin_p, cos_p.T, sin_p.T)


def _attn_kernel(qt_ref, k_ref, vt_ref, o_ref, acc_ref, s_buf, p_buf, *, tk):
    n_chunks = k_ref.shape[0] // tk
    n_pairs = n_chunks // 2
    tq = qt_ref.shape[1]

    def scores(c, slot):
        start = pl.multiple_of(c * tk, tk)
        s = _dot(k_ref[pl.ds(start, tk), :], qt_ref[...])
        s_buf[slot] = s
        return jnp.max(s, axis=0, keepdims=True)

    def softmax(slot, chunk_max, m_prev, l_prev):
        m_new = jnp.maximum(m_prev, chunk_max)
        alpha = jnp.exp2(m_prev - m_new)
        p = jnp.exp2(s_buf[slot] - m_new)
        p_buf[slot] = p.astype(BF16)
        return m_new, alpha * l_prev + jnp.sum(p, axis=0, keepdims=True), alpha

    def accumulate(c, slot, alpha):
        start = pl.multiple_of(c * tk, tk)
        pv = _dot(vt_ref[:, pl.ds(start, tk)], p_buf[slot])
        acc_ref[...] = alpha * acc_ref[...] + pv

    acc_ref[...] = jnp.zeros(acc_ref.shape, F32)
    p_buf[1] = jnp.zeros(p_buf.shape[1:], BF16)
    cm0 = scores(0, 0)

    def body(i, carry):
        m, l, alpha_prev, cm_a = carry
        c0 = 2 * i
        cm_b = scores(c0 + 1, 1)
        m, l, alpha_a = softmax(0, cm_a, m, l)
        accumulate(jnp.maximum(c0 - 1, 0), 1, alpha_prev)
        cm_a = scores(jnp.minimum(c0 + 2, n_chunks - 1), 0)
        m, l, alpha_b = softmax(1, cm_b, m, l)
        accumulate(c0, 0, alpha_a)
        return m, l, alpha_b, cm_a

    init = (jnp.full((1, tq), -jnp.inf, F32), jnp.zeros((1, tq), F32), jnp.ones((1, tq), F32), cm0)
    _, l_fin, alpha_last, _ = lax.fori_loop(0, n_pairs, body, init)
    accumulate(n_chunks - 1, 1, alpha_last)
    o_ref[...] = (acc_ref[...] / l_fin).T.astype(o_ref.dtype)


def _attention(qt, k, vt, *, tq=1024, tk=512):
    b, s, _ = k.shape
    assert (s // tk) % 2 == 0
    return pl.pallas_call(
        functools.partial(_attn_kernel, tk=tk),
        out_shape=jax.ShapeDtypeStruct((b, s, N_HEADS * V_HEAD_DIM), BF16),
        grid=(b, N_HEADS, s // tq),
        in_specs=[
            pl.BlockSpec((None, QK_PAD, tq), lambda bi, h, i: (bi, h, i)),
            pl.BlockSpec((None, s, QK_PAD), lambda bi, h, i: (bi, 0, h)),
            pl.BlockSpec((None, V_HEAD_DIM, s), lambda bi, h, i: (bi, h, 0)),
        ],
        out_specs=pl.BlockSpec((None, tq, V_HEAD_DIM), lambda bi, h, i: (bi, i, h)),
        scratch_shapes=[pltpu.VMEM((V_HEAD_DIM, tq), F32),
                        pltpu.VMEM((2, tk, tq), F32),
                        pltpu.VMEM((2, tk, tq), BF16)],
        compiler_params=_params("parallel", "parallel", "arbitrary"),
        name="attention",
    )(qt, k, vt)


def _rec_proj_kernel(x_ref, g_ref, w_ref, o_ref, h_ref):
    j = pl.program_id(1)

    @pl.when(j == 0)
    def _():
        h_ref[...] = _rms(x_ref[...], g_ref[...]).astype(BF16)
        o_ref[...] = _dot(h_ref[...], w_ref[...])

    @pl.when(j == 1)
    def _():
        o_ref[...] = jax.nn.gelu(_dot(h_ref[...], w_ref[...]))


def _rec_proj(x, g, w, *, tm=512):
    m, d = x.shape
    return pl.pallas_call(
        _rec_proj_kernel,
        out_shape=jax.ShapeDtypeStruct((m, 2 * LRU_WIDTH), F32),
        grid=(m // tm, 2),
        in_specs=[
            pl.BlockSpec((tm, d), lambda i, j: (i, 0)),
            pl.BlockSpec((1, d), lambda i, j: (0, 0)),
            pl.BlockSpec((d, LRU_WIDTH), lambda i, j: (0, j)),
        ],
        out_specs=pl.BlockSpec((tm, LRU_WIDTH), lambda i, j: (i, j)),
        scratch_shapes=[pltpu.VMEM((tm, d), BF16)],
        compiler_params=_params("parallel", "arbitrary"),
        name="rec_proj",
    )(x, g, w)


def _lru_kernel(x_ref, gg_ref, cw_ref, cb_ref, w_ref, b_ref, lam_ref, o_ref,
                xpad_ref, af_ref, uf_ref, ab_ref, ub_ref, *, seq, chunk, sub, pitch):
    halo = SUBLANES
    zeros_halo = jnp.zeros((halo, LANES), F32)
    xpad_ref[0:halo, :] = zeros_halo
    xpad_ref[halo + seq:halo + seq + halo, :] = zeros_halo
    xpad_ref[halo:halo + seq, :] = x_ref[...]

    lam = lam_ref[...]
    nl = -lam
    softplus = jnp.maximum(nl, 0.0) + jnp.log1p(jnp.exp(-jnp.abs(nl)))
    log_a_slope = (-0.5 * LRU_C) * softplus
    cw = cw_ref[...]
    cb = cb_ref[...]
    half_bias = b_ref[...]
    a_refs = (af_ref, ab_ref)
    u_refs = (uf_ref, ub_ref)

    def gates(sc, carry):
        t0 = pl.multiple_of(sc * sub, sub)
        xc = cb
        for kk in range(CONV_WIDTH):
            xc = xc + xpad_ref[pl.ds(t0 + halo - CONV_LEFT + kk, sub), :] * cw[kk:kk + 1, :]
        half_z = _dot(xc.astype(BF16), w_ref[...]) + half_bias
        half_xc = 0.5 * xc
        dst = pl.multiple_of((t0 // chunk) * pitch + t0 % chunk, SUBLANES)
        for d in range(2):
            tanh_r = jnp.tanh(half_z[:, 2 * d * LANES:(2 * d + 1) * LANES])
            tanh_i = jnp.tanh(half_z[:, (2 * d + 1) * LANES:(2 * d + 2) * LANES])
            log_a = (tanh_r + 1.0) * log_a_slope[:, d * LANES:(d + 1) * LANES]
            a = jnp.exp(log_a)
            one_minus_a2 = jnp.tanh(log_a) * (-1.0 - a * a)
            a_refs[d][pl.ds(dst, sub), :] = a
            u_refs[d][pl.ds(dst, sub), :] = jnp.sqrt(one_minus_a2) * ((tanh_i + 1.0) * half_xc)
        return carry

    lax.fori_loop(0, seq // sub, gates, 0, unroll=2)

    def strided(ref, grp, j):
        return ref.at[pl.ds(grp * SUBLANES * pitch + j, SUBLANES, stride=pitch), :]

    groups = range(SCAN_GROUPS)
    z = jnp.zeros((SUBLANES, LANES), F32)
    one = jnp.ones((SUBLANES, LANES), F32)

    def pass_a(j, carry):
        hf, pf, hb, pb = carry
        jb = chunk - 1 - j
        af = [strided(af_ref, g, j)[...] for g in groups]
        ab = [strided(ab_ref, g, jb)[...] for g in groups]
        hf = tuple(af[g] * hf[g] + strided(uf_ref, g, j)[...] for g in groups)
        pf = tuple(af[g] * pf[g] for g in groups)
        hb = tuple(ab[g] * hb[g] + strided(ub_ref, g, jb)[...] for g in groups)
        pb = tuple(ab[g] * pb[g] for g in groups)
        return hf, pf, hb, pb

    zs = tuple(z for _ in groups)
    ones = tuple(one for _ in groups)
    hf_e, pf_e, hb_e, pb_e = lax.fori_loop(0, chunk, pass_a, (zs, ones, zs, ones), unroll=2)

    row = lax.broadcasted_iota(jnp.int32, (SUBLANES, LANES), 0)
    order = [(g, c) for g in groups for c in range(SUBLANES)]
    hin_f = [z for _ in groups]
    state = jnp.zeros((1, LANES), F32)
    for g, c in order:
        hin_f[g] = jnp.where(row == c, state, hin_f[g])
        state = hf_e[g][c:c + 1, :] + pf_e[g][c:c + 1, :] * state
    hin_b = [z for _ in groups]
    state = jnp.zeros((1, LANES), F32)
    for g, c in reversed(order):
        hin_b[g] = jnp.where(row == c, state, hin_b[g])
        state = hb_e[g][c:c + 1, :] + pb_e[g][c:c + 1, :] * state

    def pass_b(j, carry):
        hf, hb = carry
        jb = chunk - 1 - j
        hf = tuple(strided(af_ref, g, j)[...] * hf[g] + strided(uf_ref, g, j)[...] for g in groups)
        hb = tuple(strided(ab_ref, g, jb)[...] * hb[g] + strided(ub_ref, g, jb)[...] for g in groups)
        for g in groups:
            strided(uf_ref, g, j)[...] = hf[g]
            strided(ub_ref, g, jb)[...] = hb[g]
        return hf, hb

    lax.fori_loop(0, chunk, pass_b, (tuple(hin_f), tuple(hin_b)), unroll=2)

    def combine(sc, carry):
        t0 = pl.multiple_of(sc * sub, sub)
        src = pl.multiple_of((t0 // chunk) * pitch + t0 % chunk, SUBLANES)
        h_rec = uf_ref[pl.ds(src, sub), :] + ub_ref[pl.ds(src, sub), :]
        o_ref[pl.ds(t0, sub), :] = (gg_ref[pl.ds(t0, sub), :] * h_rec).astype(o_ref.dtype)
        return carry

    lax.fori_loop(0, seq // sub, combine, 0)


def _lru(xg, cw, cb, wcat, bcat, lamcat):
    b, s, _ = xg.shape
    n_chunks = SUBLANES * SCAN_GROUPS
    chunk = s // n_chunks
    sub = min(256, chunk)
    assert s % n_chunks == 0 and chunk % sub == 0 and sub % SUBLANES == 0
    pitch = chunk + SUBLANES if (chunk // SUBLANES) % 2 == 0 else chunk
    scan = pltpu.VMEM((n_chunks * pitch, LANES), F32)
    return pl.pallas_call(
        functools.partial(_lru_kernel, seq=s, chunk=chunk, sub=sub, pitch=pitch),
        out_shape=jax.ShapeDtypeStruct((b, s, LRU_WIDTH), BF16),
        grid=(b, LRU_BLOCKS),
        in_specs=[
            pl.BlockSpec((None, s, LANES), lambda bi, n: (bi, 0, n)),
            pl.BlockSpec((None, s, LANES), lambda bi, n: (bi, 0, LRU_BLOCKS + n)),
            pl.BlockSpec((CONV_WIDTH, LANES), lambda bi, n: (0, n)),
            pl.BlockSpec((1, LANES), lambda bi, n: (0, n)),
            pl.BlockSpec((None, LRU_BLOCK_W, 4 * LRU_BLOCK_W), lambda bi, n: (n, 0, 0)),
            pl.BlockSpec((None, 1, 4 * LRU_BLOCK_W), lambda bi, n: (n, 0, 0)),
            pl.BlockSpec((None, 1, 2 * LRU_BLOCK_W), lambda bi, n: (n, 0, 0)),
        ],
        out_specs=pl.BlockSpec((None, s, LANES), lambda bi, n: (bi, 0, n)),
        scratch_shapes=[pltpu.VMEM((s + 2 * SUBLANES, LANES), F32), scan, scan, scan, scan],
        compiler_params=_params("parallel", "parallel"),
        name="rglru",
    )(xg, xg, cw, cb, wcat, bcat, lamcat)


def _merge_kernel(x_ref, g_ref, oa_ref, yr_ref, wga_ref, wgr_ref, woa_ref, wor_ref, wout_ref,
                  o_ref, h_ref):
    j = pl.program_id(1)

    @pl.when(j == 0)
    def _():
        x = x_ref[...]
        h_ref[...] = _rms(x, g_ref[...]).astype(BF16)
        o_ref[...] = x

    h = h_ref[...]
    ga = _sigmoid(_dot(h, wga_ref[...]))
    gr = _sigmoid(_dot(h, wgr_ref[...]))
    ya = _dot(oa_ref[...], woa_ref[...])
    yr = _dot(yr_ref[...], wor_ref[...])
    merged = (ga * ya + gr * yr).astype(BF16)
    o_ref[...] += _dot(merged, wout_ref[...])


def _merge(x, g, oa, yr, wga, wgr, woa, wor, wout, *, tm=512, tn=256):
    m, d = x.shape
    row = lambda i, j: (i, 0)
    col = lambda i, j: (0, j)
    return pl.pallas_call(
        _merge_kernel,
        out_shape=jax.ShapeDtypeStruct((m, d), F32),
        grid=(m // tm, d // tn),
        in_specs=[
            pl.BlockSpec((tm, d), row),
            pl.BlockSpec((1, d), lambda i, j: (0, 0)),
            pl.BlockSpec((tm, d), row),
            pl.BlockSpec((tm, d), row),
            pl.BlockSpec((d, tn), col),
            pl.BlockSpec((d, tn), col),
            pl.BlockSpec((d, tn), col),
            pl.BlockSpec((d, tn), col),
            pl.BlockSpec((tn, d), lambda i, j: (j, 0)),
        ],
        out_specs=pl.BlockSpec((tm, d), row),
        scratch_shapes=[pltpu.VMEM((tm, d), BF16)],
        compiler_params=_params("parallel", "arbitrary"),
        name="merge",
    )(x, g, oa, yr, wga, wgr, woa, wor, wout)


def _swap_halves(w):
    half = QK_ROPE_DIM // 2
    return jnp.concatenate([-w[..., half:], w[..., :half]], axis=-1)


def _prep_weights(p):
    row = lambda v: v.reshape(1, -1).astype(F32)
    w_in = p["w_in"]
    o = 0
    w_cq = w_in[:, o:o + Q_LORA_RANK]; o += Q_LORA_RANK
    w_ckv = w_in[:, o:o + KV_LORA_RANK]; o += KV_LORA_RANK
    w_kpe = w_in[:, o:o + QK_ROPE_DIM]; o += QK_ROPE_DIM
    w_rec = w_in[:, o:o + LRU_WIDTH]; o += LRU_WIDTH
    w_g = w_in[:, o:o + LRU_WIDTH]; o += LRU_WIDTH
    w_ga = w_in[:, o:o + D_MODEL]; o += D_MODEL
    w_gr = w_in[:, o:o + D_MODEL]

    zpad = jnp.zeros((D_MODEL, LANES - QK_ROPE_DIM), F32)
    wa = jnp.concatenate([w_cq, w_ckv, w_kpe, zpad, _swap_halves(w_kpe), zpad], axis=1)

    w_uq = p["w_uq"].reshape(Q_LORA_RANK, N_HEADS, QK_NOPE_DIM + QK_ROPE_DIM)
    uq_nope, uq_pe = w_uq[..., :QK_NOPE_DIM], w_uq[..., QK_NOPE_DIM:]
    zq = jnp.zeros((Q_LORA_RANK, N_HEADS, LANES - QK_ROPE_DIM), F32)
    wqm = jnp.concatenate([uq_nope, uq_pe, zq], axis=-1).reshape(Q_LORA_RANK, N_HEADS * QK_PAD)
    wqr = jnp.concatenate([_swap_halves(uq_pe), zq], axis=-1).reshape(Q_LORA_RANK, N_HEADS * LANES)

    w_ukv = p["w_ukv"].reshape(KV_LORA_RANK, N_HEADS, QK_NOPE_DIM + V_HEAD_DIM)
    wk = w_ukv[..., :QK_NOPE_DIM].reshape(KV_LORA_RANK, -1)
    wv = w_ukv[..., QK_NOPE_DIM:].reshape(KV_LORA_RANK, -1)

    wa_, wi_ = p["rg_w_a"], p["rg_w_i"]
    wcat = 0.5 * jnp.concatenate([wa_[0], wi_[0], wa_[1], wi_[1]], axis=-1)
    blk = lambda v: v.reshape(LRU_BLOCKS, 1, LRU_BLOCK_W)
    ba, bi = p["rg_b_a"], p["rg_b_i"]
    bcat = 0.5 * jnp.concatenate([blk(ba[0]), blk(bi[0]), blk(ba[1]), blk(bi[1])], axis=-1)
    lam = p["rg_lambda"]
    lamcat = jnp.concatenate([blk(lam[0]), blk(lam[1])], axis=-1)

    bf = lambda v: v.astype(BF16)
    return dict(
        ffn1=(row(p["ffn1_norm"]), bf(p["ffn1_w_gate"]), bf(p["ffn1_w_up"]), bf(p["ffn1_w_down"])),
        ffn2=(row(p["ffn2_norm"]), bf(p["ffn2_w_gate"]), bf(p["ffn2_w_up"]), bf(p["ffn2_w_down"])),
        mix_norm=row(p["mix_norm"]), wa=bf(wa), q_norm=row(p["q_norm"]),
        wqm_t=bf(wqm.T), wqr_t=bf(wqr.T), kv_norm=row(p["kv_norm"]), wk=bf(wk), wv_t=bf(wv.T),
        w_recg=bf(jnp.concatenate([w_rec, w_g], axis=1)),
        conv_w=p["conv_w"].astype(F32), conv_b=row(p["conv_b"]),
        wcat=bf(wcat), bcat=bcat.astype(F32), lamcat=lamcat.astype(F32),
        w_ga=bf(w_ga), w_gr=bf(w_gr), w_o_attn=bf(p["w_o_attn"]), w_o_rec=bf(p["w_o_rec"]),
        w_out=bf(p["w_out"]), final_norm=row(p["final_norm"]),
    )


def _rope_tables(seq):
    pos = jnp.arange(seq, dtype=F32)
    inv_freq = ROPE_BASE ** (-jnp.arange(0, QK_ROPE_DIM, 2, dtype=F32) / QK_ROPE_DIM)
    ang = pos[:, None] * inv_freq[None, :]
    pad = jnp.zeros((seq, LANES - QK_ROPE_DIM), F32)
    cos, sin = jnp.cos(ang), jnp.sin(ang)
    return (jnp.concatenate([cos, cos, pad], axis=1), jnp.concatenate([sin, sin, pad], axis=1))


def _encoder(x, w, cos_p, sin_p):
    b, s, d = x.shape
    xf = x.reshape(b * s, d)
    x1 = _ffn(xf, *w["ffn1"], w["final_norm"], final_norm=False)
    qt, k, vt = _attn_prep(x1.reshape(b, s, d), w["mix_norm"], w["wa"], w["q_norm"], w["wqm_t"],
                           w["wqr_t"], w["kv_norm"], w["wk"], w["wv_t"], cos_p, sin_p)
    o_attn = _attention(qt, k, vt)
    xg = _rec_proj(x1, w["mix_norm"], w["w_recg"])
    y_rec = _lru(xg.reshape(b, s, -1), w["conv_w"], w["conv_b"], w["wcat"], w["bcat"], w["lamcat"])
    x2 = _merge(x1, w["mix_norm"], o_attn.reshape(b * s, -1), y_rec.reshape(b * s, -1),
                w["w_ga"], w["w_gr"], w["w_o_attn"], w["w_o_rec"], w["w_out"])
    y = _ffn(x2, *w["ffn2"], w["final_norm"], final_norm=True)
    return y.reshape(b, s, d)


def kernel(x_prompt, x_sample, ffn1_norm, ffn1_w_gate, ffn1_w_up, ffn1_w_down, mix_norm, w_in, q_norm, w_uq, kv_norm, w_ukv, w_o_attn, conv_w, conv_b, rg_w_a, rg_b_a, rg_w_i, rg_b_i, rg_lambda, w_o_rec, w_out, ffn2_norm, ffn2_w_gate, ffn2_w_up, ffn2_w_down, final_norm):
    layer0 = dict(
        ffn1_norm=ffn1_norm[0], ffn1_w_gate=ffn1_w_gate[0], ffn1_w_up=ffn1_w_up[0],
        ffn1_w_down=ffn1_w_down[0], mix_norm=mix_norm[0], w_in=w_in[0], q_norm=q_norm[0],
        w_uq=w_uq[0], kv_norm=kv_norm[0], w_ukv=w_ukv[0], w_o_attn=w_o_attn[0],
        conv_w=conv_w[0], conv_b=conv_b[0], rg_w_a=rg_w_a[0], rg_b_a=rg_b_a[0],
        rg_w_i=rg_w_i[0], rg_b_i=rg_b_i[0], rg_lambda=rg_lambda[0], w_o_rec=w_o_rec[0],
        w_out=w_out[0], ffn2_norm=ffn2_norm[0], ffn2_w_gate=ffn2_w_gate[0],
        ffn2_w_up=ffn2_w_up[0], ffn2_w_down=ffn2_w_down[0], final_norm=final_norm)
    w = _prep_weights(layer0)
    assert x_prompt.shape[1] == x_sample.shape[1]
    cos_p, sin_p = _rope_tables(x_prompt.shape[1])
    return (_encoder(x_prompt, w, cos_p, sin_p), _encoder(x_sample, w, cos_p, sin_p))
```

```python
import functools

import jax
import jax.numpy as jnp
from jax import lax
from jax.experimental import pallas as pl
from jax.experimental.pallas import tpu as pltpu

D_MODEL = 2048
N_HEADS = 16
QK_NOPE_DIM = 128
QK_ROPE_DIM = 64
V_HEAD_DIM = 128
Q_LORA_RANK = 512
KV_LORA_RANK = 512
ROPE_BASE = 10000.0
LRU_WIDTH = D_MODEL
LRU_BLOCKS = 16
LRU_BLOCK_W = LRU_WIDTH // LRU_BLOCKS
CONV_WIDTH = 4
CONV_LEFT = 2
LRU_C = 8.0
RMS_EPS = 1e-6

LANES = 128
SUBLANES = 8
QK_PAD = 256
V_AUG = V_HEAD_DIM + 16
VMEM_LIMIT = 56 * 1024 * 1024
SCAN_GROUPS = 4

BF16 = jnp.bfloat16
F32 = jnp.float32


def _rms(x, g):
    ms = jnp.mean(x * x, axis=-1, keepdims=True)
    return x * lax.rsqrt(ms + RMS_EPS) * g


def _sigmoid(x):
    return 0.5 * jnp.tanh(0.5 * x) + 0.5


def _dot(a, b):
    return jnp.dot(a, b, preferred_element_type=F32)


def _params(*sem):
    return pltpu.CompilerParams(dimension_semantics=sem, vmem_limit_bytes=VMEM_LIMIT)


def _resident(shape, index_map):
    return pl.BlockSpec(shape, index_map, pipeline_mode=pl.Buffered(1))


def _ffn_kernel(x_ref, g_ref, wg_ref, wu_ref, wd_ref, fn_ref, o_ref, h_ref, *, final_norm):
    j = pl.program_id(1)

    @pl.when(j == 0)
    def _():
        x = x_ref[...]
        h_ref[...] = _rms(x, g_ref[...]).astype(BF16)
        o_ref[...] = x

    h = h_ref[...]
    a = _dot(h, wg_ref[...])
    b = _dot(h, wu_ref[...])
    act = (a * _sigmoid(a) * b).astype(BF16)
    o_ref[...] += 0.5 * _dot(act, wd_ref[...])

    if final_norm:
        @pl.when(j == pl.num_programs(1) - 1)
        def _():
            o_ref[...] = _rms(o_ref[...], fn_ref[...])


def _ffn(x, g, wg, wu, wd, fn, *, final_norm, tm=512, tf=512):
    m, d = x.shape
    ff = wg.shape[1]
    return pl.pallas_call(
        functools.partial(_ffn_kernel, final_norm=final_norm),
        out_shape=jax.ShapeDtypeStruct((m, d), F32),
        grid=(m // tm, ff // tf),
        in_specs=[
            pl.BlockSpec((tm, d), lambda i, j: (i, 0)),
            pl.BlockSpec((1, d), lambda i, j: (0, 0)),
            pl.BlockSpec((d, tf), lambda i, j: (0, j)),
            pl.BlockSpec((d, tf), lambda i, j: (0, j)),
            pl.BlockSpec((tf, d), lambda i, j: (j, 0)),
            pl.BlockSpec((1, d), lambda i, j: (0, 0)),
        ],
        out_specs=pl.BlockSpec((tm, d), lambda i, j: (i, 0)),
        scratch_shapes=[pltpu.VMEM((tm, d), BF16)],
        compiler_params=_params("parallel", "arbitrary"),
        name="ffn_final" if final_norm else "ffn",
    )(x, g, wg, wu, wd, fn)


def _attn_prep_kernel(x_ref, g_ref, wa_ref, qn_ref, wqm_ref, wqr_ref, kvn_ref, wk_ref, wv_ref,
                      cos_ref, sin_ref, cost_ref, sint_ref, qt_ref, k_ref, vt_ref, *, scale):
    h = _rms(x_ref[...], g_ref[...]).astype(BF16)
    p = _dot(h, wa_ref[...])
    cq = p[:, :Q_LORA_RANK]
    ckv = p[:, Q_LORA_RANK:Q_LORA_RANK + KV_LORA_RANK]
    base = Q_LORA_RANK + KV_LORA_RANK
    kp_main = p[:, base:base + LANES]
    kp_rot = p[:, base + LANES:base + 2 * LANES]
    kpe = (kp_main * cos_ref[...] + kp_rot * sin_ref[...]).astype(BF16)
    cqn_t = _rms(cq, qn_ref[...]).T.astype(BF16)
    ckvn = _rms(ckv, kvn_ref[...])
    ckvn_t = ckvn.T.astype(BF16)
    qm_t = _dot(wqm_ref[...], cqn_t)
    qr_t = _dot(wqr_ref[...], cqn_t)
    k_nope = _dot(ckvn.astype(BF16), wk_ref[...])
    v_t = _dot(wv_ref[...], ckvn_t)
    ones_rows = jnp.ones((V_AUG - V_HEAD_DIM, v_t.shape[1]), BF16)
    cos_t = cost_ref[...]
    sin_t = sint_ref[...]
    for hd in range(N_HEADS):
        c0 = hd * QK_PAD
        r0 = hd * LANES
        qt_ref[c0:c0 + LANES, :] = (qm_t[c0:c0 + LANES, :] * scale).astype(BF16)
        qt_ref[c0 + LANES:c0 + QK_PAD, :] = (
            (qm_t[c0 + LANES:c0 + QK_PAD, :] * cos_t + qr_t[r0:r0 + LANES, :] * sin_t) * scale
        ).astype(BF16)
        k_ref[:, c0:c0 + LANES] = k_nope[:, r0:r0 + LANES].astype(BF16)
        k_ref[:, c0 + LANES:c0 + QK_PAD] = kpe
        v0 = hd * V_AUG
        vt_ref[v0:v0 + V_HEAD_DIM, :] = v_t[r0:r0 + V_HEAD_DIM, :].astype(BF16)
        vt_ref[v0 + V_HEAD_DIM:v0 + V_AUG, :] = ones_rows


def _attn_prep(x, g, wa, qn, wqm_t, wqr_t, kvn, wk, wv_t, cos_p, sin_p, *, tm=256):
    b, s, d = x.shape
    scale = float((QK_NOPE_DIM + QK_ROPE_DIM) ** -0.5 * 1.4426950408889634)
    const = lambda bi, i: (0, 0)
    return pl.pallas_call(
        functools.partial(_attn_prep_kernel, scale=scale),
        out_shape=(jax.ShapeDtypeStruct((b, N_HEADS * QK_PAD, s), BF16),
                   jax.ShapeDtypeStruct((b, s, N_HEADS * QK_PAD), BF16),
                   jax.ShapeDtypeStruct((b, N_HEADS * V_AUG, s), BF16)),
        grid=(b, s // tm),
        in_specs=[
            pl.BlockSpec((None, tm, d), lambda bi, i: (bi, i, 0)),
            pl.BlockSpec((1, d), const),
            _resident(wa.shape, const),
            pl.BlockSpec((1, Q_LORA_RANK), const),
            _resident(wqm_t.shape, const),
            _resident(wqr_t.shape, const),
            pl.BlockSpec((1, KV_LORA_RANK), const),
            _resident(wk.shape, const),
            _resident(wv_t.shape, const),
            pl.BlockSpec((tm, LANES), lambda bi, i: (i, 0)),
            pl.BlockSpec((tm, LANES), lambda bi, i: (i, 0)),
            pl.BlockSpec((LANES, tm), lambda bi, i: (0, i)),
            pl.BlockSpec((LANES, tm), lambda bi, i: (0, i)),
        ],
        out_specs=(pl.BlockSpec((None, N_HEADS * QK_PAD, tm), lambda bi, i: (bi, 0, i)),
                   pl.BlockSpec((None, tm, N_HEADS * QK_PAD), lambda bi, i: (bi, i, 0)),
                   pl.BlockSpec((None, N_HEADS * V_AUG, tm), lambda bi, i: (bi, 0, i))),
        compiler_params=_params("parallel", "parallel"),
        name="attn_prep",
    )(x, g, wa, qn, wqm_t, wqr_t, kvn, wk, wv_t, cos_p, sin_p, cos_p.T, sin_p.T)


def _attn_kernel(qt_ref, k_ref, vt_ref, o_ref, acc_ref, s_buf, p_buf, *, tk):
    n_chunks = k_ref.shape[0] // tk
    n_pairs = n_chunks // 2
    tq = qt_ref.shape[1]

    def scores(c, slot):
        start = pl.multiple_of(c * tk, tk)
        s = _dot(k_ref[pl.ds(start, tk), :], qt_ref[...])
        s_buf[slot] = s
        return jnp.max(s, axis=0, keepdims=True)

    def softmax(slot, chunk_max, m_prev):
        m_new = jnp.maximum(m_prev, chunk_max)
        alpha = jnp.exp2(m_prev - m_new)
        p_buf[slot] = jnp.exp2((s_buf[slot] - m_new).astype(BF16))
        return m_new, alpha

    def accumulate(c, slot, alpha):
        start = pl.multiple_of(c * tk, tk)
        pv = _dot(vt_ref[:, pl.ds(start, tk)], p_buf[slot])
        acc_ref[...] = alpha * acc_ref[...] + pv

    acc_ref[...] = jnp.zeros(acc_ref.shape, F32)
    p_buf[1] = jnp.zeros(p_buf.shape[1:], BF16)
    cm0 = scores(0, 0)

    def body(i, carry):
        m, alpha_prev, cm_a = carry
        c0 = 2 * i
        cm_b = scores(c0 + 1, 1)
        m, alpha_a = softmax(0, cm_a, m)
        accumulate(jnp.maximum(c0 - 1, 0), 1, alpha_prev)
        cm_a = scores(jnp.minimum(c0 + 2, n_chunks - 1), 0)
        m, alpha_b = softmax(1, cm_b, m)
        accumulate(c0, 0, alpha_a)
        return m, alpha_b, cm_a

    init = (jnp.full((1, tq), -jnp.inf, F32), jnp.ones((1, tq), F32), cm0)
    _, alpha_last, _ = lax.fori_loop(0, n_pairs, body, init)
    accumulate(n_chunks - 1, 1, alpha_last)
    acc = acc_ref[...]
    o = acc[:V_HEAD_DIM, :] / acc[V_HEAD_DIM:V_HEAD_DIM + 1, :]
    o_ref[...] = o.T.astype(o_ref.dtype)


def _attention(qt, k, vt, *, tq=2048, tk=512):
    b, s, _ = k.shape
    assert (s // tk) % 2 == 0
    return pl.pallas_call(
        functools.partial(_attn_kernel, tk=tk),
        out_shape=jax.ShapeDtypeStruct((b, s, N_HEADS * V_HEAD_DIM), BF16),
        grid=(b, N_HEADS, s // tq),
        in_specs=[
            pl.BlockSpec((None, QK_PAD, tq), lambda bi, h, i: (bi, h, i)),
            pl.BlockSpec((None, s, QK_PAD), lambda bi, h, i: (bi, 0, h)),
            pl.BlockSpec((None, V_AUG, s), lambda bi, h, i: (bi, h, 0)),
        ],
        out_specs=pl.BlockSpec((None, tq, V_HEAD_DIM), lambda bi, h, i: (bi, i, h)),
        scratch_shapes=[pltpu.VMEM((V_AUG, tq), F32),
                        pltpu.VMEM((2, tk, tq), F32),
                        pltpu.VMEM((2, tk, tq), BF16)],
        compiler_params=_params("parallel", "parallel", "arbitrary"),
        name="attention",
    )(qt, k, vt)


def _rec_proj_kernel(x_ref, g_ref, w_ref, o_ref, h_ref):
    j = pl.program_id(1)

    @pl.when(j == 0)
    def _():
        h_ref[...] = _rms(x_ref[...], g_ref[...]).astype(BF16)
        o_ref[...] = _dot(h_ref[...], w_ref[...])

    @pl.when(j == 1)
    def _():
        o_ref[...] = jax.nn.gelu(_dot(h_ref[...], w_ref[...]))


def _rec_proj(x, g, w, *, tm=512):
    m, d = x.shape
    return pl.pallas_call(
        _rec_proj_kernel,
        out_shape=jax.ShapeDtypeStruct((m, 2 * LRU_WIDTH), F32),
        grid=(m // tm, 2),
        in_specs=[
            pl.BlockSpec((tm, d), lambda i, j: (i, 0)),
            pl.BlockSpec((1, d), lambda i, j: (0, 0)),
            pl.BlockSpec((d, LRU_WIDTH), lambda i, j: (0, j)),
        ],
        out_specs=pl.BlockSpec((tm, LRU_WIDTH), lambda i, j: (i, j)),
        scratch_shapes=[pltpu.VMEM((tm, d), BF16)],
        compiler_params=_params("parallel", "arbitrary"),
        name="rec_proj",
    )(x, g, w)


def _lru_kernel(x_ref, gg_ref, cw_ref, cb_ref, w_ref, b_ref, lam_ref, o_ref,
                xpad_ref, af_ref, uf_ref, ab_ref, ub_ref, *, seq, chunk, sub, pitch):
    halo = SUBLANES
    zeros_halo = jnp.zeros((halo, LANES), F32)
    xpad_ref[0:halo, :] = zeros_halo
    xpad_ref[halo + seq:halo + seq + halo, :] = zeros_halo
    xpad_ref[halo:halo + seq, :] = x_ref[...]

    lam = lam_ref[...]
    nl = -lam
    softplus = jnp.maximum(nl, 0.0) + jnp.log1p(jnp.exp(-jnp.abs(nl)))
    log_a_slope = (-0.5 * LRU_C) * softplus
    cw = cw_ref[...]
    cb = cb_ref[...]
    half_bias = b_ref[...]
    a_refs = (af_ref, ab_ref)
    u_refs = (uf_ref, ub_ref)

    def gates(sc, carry):
        t0 = pl.multiple_of(sc * sub, sub)
        xc = cb
        for kk in range(CONV_WIDTH):
            xc = xc + xpad_ref[pl.ds(t0 + halo - CONV_LEFT + kk, sub), :] * cw[kk:kk + 1, :]
        half_z = _dot(xc.astype(BF16), w_ref[...]) + half_bias
        half_xc = 0.5 * xc
        dst = pl.multiple_of((t0 // chunk) * pitch + t0 % chunk, SUBLANES)
        for d in range(2):
            tanh_r = jnp.tanh(half_z[:, 2 * d * LANES:(2 * d + 1) * LANES])
            tanh_i = jnp.tanh(half_z[:, (2 * d + 1) * LANES:(2 * d + 2) * LANES])
            log_a = (tanh_r + 1.0) * log_a_slope[:, d * LANES:(d + 1) * LANES]
            a = jnp.exp(log_a)
            one_minus_a2 = jnp.tanh(log_a) * (-1.0 - a * a)
            a_refs[d][pl.ds(dst, sub), :] = a
            u_refs[d][pl.ds(dst, sub), :] = jnp.sqrt(one_minus_a2) * ((tanh_i + 1.0) * half_xc)
        return carry

    lax.fori_loop(0, seq // sub, gates, 0, unroll=2)

    def strided(ref, grp, j):
        return ref.at[pl.ds(grp * SUBLANES * pitch + j, SUBLANES, stride=pitch), :]

    groups = range(SCAN_GROUPS)
    z = jnp.zeros((SUBLANES, LANES), F32)
    one = jnp.ones((SUBLANES, LANES), F32)

    def pass_a(j, carry):
        hf, pf, hb, pb = carry
        jb = chunk - 1 - j
        af = [strided(af_ref, g, j)[...] for g in groups]
        ab = [strided(ab_ref, g, jb)[...] for g in groups]
        hf = tuple(af[g] * hf[g] + strided(uf_ref, g, j)[...] for g in groups)
        pf = tuple(af[g] * pf[g] for g in groups)
        hb = tuple(ab[g] * hb[g] + strided(ub_ref, g, jb)[...] for g in groups)
        pb = tuple(ab[g] * pb[g] for g in groups)
        return hf, pf, hb, pb

    zs = tuple(z for _ in groups)
    ones = tuple(one for _ in groups)
    hf_e, pf_e, hb_e, pb_e = lax.fori_loop(0, chunk, pass_a, (zs, ones, zs, ones), unroll=2)

    row = lax.broadcasted_iota(jnp.int32, (SUBLANES, LANES), 0)
    order = [(g, c) for g in groups for c in range(SUBLANES)]
    hin_f = [z for _ in groups]
    state = jnp.zeros((1, LANES), F32)
    for g, c in order:
        hin_f[g] = jnp.where(row == c, state, hin_f[g])
        state = hf_e[g][c:c + 1, :] + pf_e[g][c:c + 1, :] * state
    hin_b = [z for _ in groups]
    state = jnp.zeros((1, LANES), F32)
    for g, c in reversed(order):
        hin_b[g] = jnp.where(row == c, state, hin_b[g])
        state = hb_e[g][c:c + 1, :] + pb_e[g][c:c + 1, :] * state

    def pass_b(j, carry):
        hf, hb = carry
        jb = chunk - 1 - j
        hf = tuple(strided(af_ref, g, j)[...] * hf[g] + strided(uf_ref, g, j)[...] for g in groups)
        hb = tuple(strided(ab_ref, g, jb)[...] * hb[g] + strided(ub_ref, g, jb)[...] for g in groups)
        for g in groups:
            strided(uf_ref, g, j)[...] = hf[g]
            strided(ub_ref, g, jb)[...] = hb[g]
        return hf, hb

    lax.fori_loop(0, chunk, pass_b, (tuple(hin_f), tuple(hin_b)), unroll=2)

    def combine(sc, carry):
        t0 = pl.multiple_of(sc * sub, sub)
        src = pl.multiple_of((t0 // chunk) * pitch + t0 % chunk, SUBLANES)
        h_rec = uf_ref[pl.ds(src, sub), :] + ub_ref[pl.ds(src, sub), :]
        o_ref[pl.ds(t0, sub), :] = (gg_ref[pl.ds(t0, sub), :] * h_rec).astype(o_ref.dtype)
        return carry

    lax.fori_loop(0, seq // sub, combine, 0)


def _lru(xg, cw, cb, wcat, bcat, lamcat):
    b, s, _ = xg.shape
    n_chunks = SUBLANES * SCAN_GROUPS
    chunk = s // n_chunks
    sub = min(256, chunk)
    assert s % n_chunks == 0 and chunk % sub == 0 and sub % SUBLANES == 0
    pitch = chunk + SUBLANES if (chunk // SUBLANES) % 2 == 0 else chunk
    scan = pltpu.VMEM((n_chunks * pitch, LANES), F32)
    return pl.pallas_call(
        functools.partial(_lru_kernel, seq=s, chunk=chunk, sub=sub, pitch=pitch),
        out_shape=jax.ShapeDtypeStruct((b, s, LRU_WIDTH), BF16),
        grid=(b, LRU_BLOCKS),
        in_specs=[
            pl.BlockSpec((None, s, LANES), lambda bi, n: (bi, 0, n)),
            pl.BlockSpec((None, s, LANES), lambda bi, n: (bi, 0, LRU_BLOCKS + n)),
            pl.BlockSpec((CONV_WIDTH, LANES), lambda bi, n: (0, n)),
            pl.BlockSpec((1, LANES), lambda bi, n: (0, n)),
            pl.BlockSpec((None, LRU_BLOCK_W, 4 * LRU_BLOCK_W), lambda bi, n: (n, 0, 0)),
            pl.BlockSpec((None, 1, 4 * LRU_BLOCK_W), lambda bi, n: (n, 0, 0)),
            pl.BlockSpec((None, 1, 2 * LRU_BLOCK_W), lambda bi, n: (n, 0, 0)),
        ],
        out_specs=pl.BlockSpec((None, s, LANES), lambda bi, n: (bi, 0, n)),
        scratch_shapes=[pltpu.VMEM((s + 2 * SUBLANES, LANES), F32), scan, scan, scan, scan],
        compiler_params=_params("parallel", "parallel"),
        name="rglru",
    )(xg, xg, cw, cb, wcat, bcat, lamcat)


def _merge_kernel(x_ref, g_ref, oa_ref, yr_ref, wga_ref, wgr_ref, woa_ref, wor_ref, wout_ref,
                  o_ref, h_ref):
    j = pl.program_id(1)

    @pl.when(j == 0)
    def _():
        x = x_ref[...]
        h_ref[...] = _rms(x, g_ref[...]).astype(BF16)
        o_ref[...] = x

    h = h_ref[...]
    ga = _sigmoid(_dot(h, wga_ref[...]))
    gr = _sigmoid(_dot(h, wgr_ref[...]))
    ya = _dot(oa_ref[...], woa_ref[...])
    yr = _dot(yr_ref[...], wor_ref[...])
    merged = (ga * ya + gr * yr).astype(BF16)
    o_ref[...] += _dot(merged, wout_ref[...])


def _merge(x, g, oa, yr, wga, wgr, woa, wor, wout, *, tm=512, tn=512):
    m, d = x.shape
    row = lambda i, j: (i, 0)
    col = lambda i, j: (0, j)
    return pl.pallas_call(
        _merge_kernel,
        out_shape=jax.ShapeDtypeStruct((m, d), F32),
        grid=(m // tm, d // tn),
        in_specs=[
            pl.BlockSpec((tm, d), row),
            pl.BlockSpec((1, d), lambda i, j: (0, 0)),
            pl.BlockSpec((tm, d), row),
            pl.BlockSpec((tm, d), row),
            pl.BlockSpec((d, tn), col),
            pl.BlockSpec((d, tn), col),
            pl.BlockSpec((d, tn), col),
            pl.BlockSpec((d, tn), col),
            pl.BlockSpec((tn, d), lambda i, j: (j, 0)),
        ],
        out_specs=pl.BlockSpec((tm, d), row),
        scratch_shapes=[pltpu.VMEM((tm, d), BF16)],
        compiler_params=_params("parallel", "arbitrary"),
        name="merge",
    )(x, g, oa, yr, wga, wgr, woa, wor, wout)


def _swap_halves(w):
    half = QK_ROPE_DIM // 2
    return jnp.concatenate([-w[..., half:], w[..., :half]], axis=-1)


def _prep_weights(p):
    row = lambda v: v.reshape(1, -1).astype(F32)
    w_in = p["w_in"]
    o = 0
    w_cq = w_in[:, o:o + Q_LORA_RANK]; o += Q_LORA_RANK
    w_ckv = w_in[:, o:o + KV_LORA_RANK]; o += KV_LORA_RANK
    w_kpe = w_in[:, o:o + QK_ROPE_DIM]; o += QK_ROPE_DIM
    w_rec = w_in[:, o:o + LRU_WIDTH]; o += LRU_WIDTH
    w_g = w_in[:, o:o + LRU_WIDTH]; o += LRU_WIDTH
    w_ga = w_in[:, o:o + D_MODEL]; o += D_MODEL
    w_gr = w_in[:, o:o + D_MODEL]

    zpad = jnp.zeros((D_MODEL, LANES - QK_ROPE_DIM), F32)
    wa = jnp.concatenate([w_cq, w_ckv, w_kpe, zpad, _swap_halves(w_kpe), zpad], axis=1)

    w_uq = p["w_uq"].reshape(Q_LORA_RANK, N_HEADS, QK_NOPE_DIM + QK_ROPE_DIM)
    uq_nope, uq_pe = w_uq[..., :QK_NOPE_DIM], w_uq[..., QK_NOPE_DIM:]
    zq = jnp.zeros((Q_LORA_RANK, N_HEADS, LANES - QK_ROPE_DIM), F32)
    wqm = jnp.concatenate([uq_nope, uq_pe, zq], axis=-1).reshape(Q_LORA_RANK, N_HEADS * QK_PAD)
    wqr = jnp.concatenate([_swap_halves(uq_pe), zq], axis=-1).reshape(Q_LORA_RANK, N_HEADS * LANES)

    w_ukv = p["w_ukv"].reshape(KV_LORA_RANK, N_HEADS, QK_NOPE_DIM + V_HEAD_DIM)
    wk = w_ukv[..., :QK_NOPE_DIM].reshape(KV_LORA_RANK, -1)
    wv = w_ukv[..., QK_NOPE_DIM:].reshape(KV_LORA_RANK, -1)

    wa_, wi_ = p["rg_w_a"], p["rg_w_i"]
    wcat = 0.5 * jnp.concatenate([wa_[0], wi_[0], wa_[1], wi_[1]], axis=-1)
    blk = lambda v: v.reshape(LRU_BLOCKS, 1, LRU_BLOCK_W)
    ba, bi = p["rg_b_a"], p["rg_b_i"]
    bcat = 0.5 * jnp.concatenate([blk(ba[0]), blk(bi[0]), blk(ba[1]), blk(bi[1])], axis=-1)
    lam = p["rg_lambda"]
    lamcat = jnp.concatenate([blk(lam[0]), blk(lam[1])], axis=-1)

    bf = lambda v: v.astype(BF16)
    return dict(
        ffn1=(row(p["ffn1_norm"]), bf(p["ffn1_w_gate"]), bf(p["ffn1_w_up"]), bf(p["ffn1_w_down"])),
        ffn2=(row(p["ffn2_norm"]), bf(p["ffn2_w_gate"]), bf(p["ffn2_w_up"]), bf(p["ffn2_w_down"])),
        mix_norm=row(p["mix_norm"]), wa=bf(wa), q_norm=row(p["q_norm"]),
        wqm_t=bf(wqm.T), wqr_t=bf(wqr.T), kv_norm=row(p["kv_norm"]), wk=bf(wk), wv_t=bf(wv.T),
        w_recg=bf(jnp.concatenate([w_rec, w_g], axis=1)),
        conv_w=p["conv_w"].astype(F32), conv_b=row(p["conv_b"]),
        wcat=bf(wcat), bcat=bcat.astype(F32), lamcat=lamcat.astype(F32),
        w_ga=bf(w_ga), w_gr=bf(w_gr), w_o_attn=bf(p["w_o_attn"]), w_o_rec=bf(p["w_o_rec"]),
        w_out=bf(p["w_out"]), final_norm=row(p["final_norm"]),
    )


def _rope_tables(seq):
    pos = jnp.arange(seq, dtype=F32)
    inv_freq = ROPE_BASE ** (-jnp.arange(0, QK_ROPE_DIM, 2, dtype=F32) / QK_ROPE_DIM)
    ang = pos[:, None] * inv_freq[None, :]
    pad = jnp.zeros((seq, LANES - QK_ROPE_DIM), F32)
    cos, sin = jnp.cos(ang), jnp.sin(ang)
    return (jnp.concatenate([cos, cos, pad], axis=1), jnp.concatenate([sin, sin, pad], axis=1))


def _encoder(x, w, cos_p, sin_p):
    b, s, d = x.shape
    xf = x.reshape(b * s, d)
    x1 = _ffn(xf, *w["ffn1"], w["final_norm"], final_norm=False)
    qt, k, vt = _attn_prep(x1.reshape(b, s, d), w["mix_norm"], w["wa"], w["q_norm"], w["wqm_t"],
                           w["wqr_t"], w["kv_norm"], w["wk"], w["wv_t"], cos_p, sin_p)
    o_attn = _attention(qt, k, vt)
    xg = _rec_proj(x1, w["mix_norm"], w["w_recg"])
    y_rec = _lru(xg.reshape(b, s, -1), w["conv_w"], w["conv_b"], w["wcat"], w["bcat"], w["lamcat"])
    x2 = _merge(x1, w["mix_norm"], o_attn.reshape(b * s, -1), y_rec.reshape(b * s, -1),
                w["w_ga"], w["w_gr"], w["w_o_attn"], w["w_o_rec"], w["w_out"])
    y = _ffn(x2, *w["ffn2"], w["final_norm"], final_norm=True)
    return y.reshape(b, s, d)


def kernel(x_prompt, x_sample, ffn1_norm, ffn1_w_gate, ffn1_w_up, ffn1_w_down, mix_norm, w_in, q_norm, w_uq, kv_norm, w_ukv, w_o_attn, conv_w, conv_b, rg_w_a, rg_b_a, rg_w_i, rg_b_i, rg_lambda, w_o_rec, w_out, ffn2_norm, ffn2_w_gate, ffn2_w_up, ffn2_w_down, final_norm):
    layer0 = dict(
        ffn1_norm=ffn1_norm[0], ffn1_w_gate=ffn1_w_gate[0], ffn1_w_up=ffn1_w_up[0],
        ffn1_w_down=ffn1_w_down[0], mix_norm=mix_norm[0], w_in=w_in[0], q_norm=q_norm[0],
        w_uq=w_uq[0], kv_norm=kv_norm[0], w_ukv=w_ukv[0], w_o_attn=w_o_attn[0],
        conv_w=conv_w[0], conv_b=conv_b[0], rg_w_a=rg_w_a[0], rg_b_a=rg_b_a[0],
        rg_w_i=rg_w_i[0], rg_b_i=rg_b_i[0], rg_lambda=rg_lambda[0], w_o_rec=w_o_rec[0],
        w_out=w_out[0], ffn2_norm=ffn2_norm[0], ffn2_w_gate=ffn2_w_gate[0],
        ffn2_w_up=ffn2_w_up[0], ffn2_w_down=ffn2_w_down[0], final_norm=final_norm)
    w = _prep_weights(layer0)
    assert x_prompt.shape[1] == x_sample.shape[1]
    cos_p, sin_p = _rope_tables(x_prompt.shape[1])
    return (_encoder(x_prompt, w, cos_p, sin_p), _encoder(x_sample, w, cos_p, sin_p))
```

```python
import functools

import jax
import jax.numpy as jnp
from jax import lax
from jax.experimental import pallas as pl
from jax.experimental.pallas import tpu as pltpu

D_MODEL = 2048
N_HEADS = 16
QK_NOPE_DIM = 128
QK_ROPE_DIM = 64
V_HEAD_DIM = 128
Q_LORA_RANK = 512
KV_LORA_RANK = 512
ROPE_BASE = 10000.0
LRU_WIDTH = D_MODEL
LRU_BLOCKS = 16
LRU_BLOCK_W = LRU_WIDTH // LRU_BLOCKS
CONV_WIDTH = 4
CONV_LEFT = 2
LRU_C = 8.0
RMS_EPS = 1e-6

LANES = 128
SUBLANES = 8
QK_PAD = 256
V_AUG = V_HEAD_DIM + 16
VMEM_LIMIT = 56 * 1024 * 1024
SCAN_GROUPS = 4

BF16 = jnp.bfloat16
F32 = jnp.float32


def _rms(x, g):
    ms = jnp.mean(x * x, axis=-1, keepdims=True)
    return x * lax.rsqrt(ms + RMS_EPS) * g


def _sigmoid(x):
    return 0.5 * jnp.tanh(0.5 * x) + 0.5


def _dot(a, b):
    return jnp.dot(a, b, preferred_element_type=F32)


def _params(*sem):
    return pltpu.CompilerParams(dimension_semantics=sem, vmem_limit_bytes=VMEM_LIMIT)


def _resident(shape, index_map):
    return pl.BlockSpec(shape, index_map, pipeline_mode=pl.Buffered(1))


def _ffn_kernel(x_ref, g_ref, wg_ref, wu_ref, wd_ref, fn_ref, o_ref, h_ref, *, final_norm):
    j = pl.program_id(1)

    @pl.when(j == 0)
    def _():
        x = x_ref[...]
        h_ref[...] = _rms(x, g_ref[...]).astype(BF16)
        o_ref[...] = x

    h = h_ref[...]
    a = _dot(h, wg_ref[...])
    b = _dot(h, wu_ref[...])
    act = (a * _sigmoid(a) * b).astype(BF16)
    o_ref[...] += 0.5 * _dot(act, wd_ref[...])

    if final_norm:
        @pl.when(j == pl.num_programs(1) - 1)
        def _():
            o_ref[...] = _rms(o_ref[...], fn_ref[...])


def _ffn(x, g, wg, wu, wd, fn, *, final_norm, tm=1024, tf=512):
    m, d = x.shape
    ff = wg.shape[1]
    return pl.pallas_call(
        functools.partial(_ffn_kernel, final_norm=final_norm),
        out_shape=jax.ShapeDtypeStruct((m, d), F32),
        grid=(m // tm, ff // tf),
        in_specs=[
            pl.BlockSpec((tm, d), lambda i, j: (i, 0)),
            pl.BlockSpec((1, d), lambda i, j: (0, 0)),
            pl.BlockSpec((d, tf), lambda i, j: (0, j)),
            pl.BlockSpec((d, tf), lambda i, j: (0, j)),
            pl.BlockSpec((tf, d), lambda i, j: (j, 0)),
            pl.BlockSpec((1, d), lambda i, j: (0, 0)),
        ],
        out_specs=pl.BlockSpec((tm, d), lambda i, j: (i, 0)),
        scratch_shapes=[pltpu.VMEM((tm, d), BF16)],
        compiler_params=_params("parallel", "arbitrary"),
        name="ffn_final" if final_norm else "ffn",
    )(x, g, wg, wu, wd, fn)


def _attn_prep_kernel(x_ref, g_ref, wa_ref, qn_ref, wqm_ref, wqr_ref, kvn_ref, wk_ref, wv_ref,
                      cos_ref, sin_ref, cost_ref, sint_ref, qt_ref, k_ref, vt_ref, *, scale):
    h = _rms(x_ref[...], g_ref[...]).astype(BF16)
    p = _dot(h, wa_ref[...])
    cq = p[:, :Q_LORA_RANK]
    ckv = p[:, Q_LORA_RANK:Q_LORA_RANK + KV_LORA_RANK]
    base = Q_LORA_RANK + KV_LORA_RANK
    kp_main = p[:, base:base + LANES]
    kp_rot = p[:, base + LANES:base + 2 * LANES]
    kpe = (kp_main * cos_ref[...] + kp_rot * sin_ref[...]).astype(BF16)
    cqn_t = _rms(cq, qn_ref[...]).T.astype(BF16)
    ckvn = _rms(ckv, kvn_ref[...])
    ckvn_t = ckvn.T.astype(BF16)
    qm_t = _dot(wqm_ref[...], cqn_t)
    qr_t = _dot(wqr_ref[...], cqn_t)
    k_nope = _dot(ckvn.astype(BF16), wk_ref[...])
    v_t = _dot(wv_ref[...], ckvn_t)
    ones_rows = jnp.ones((V_AUG - V_HEAD_DIM, v_t.shape[1]), BF16)
    cos_t = cost_ref[...]
    sin_t = sint_ref[...]
    for hd in range(N_HEADS):
        c0 = hd * QK_PAD
        r0 = hd * LANES
        qt_ref[c0:c0 + LANES, :] = (qm_t[c0:c0 + LANES, :] * scale).astype(BF16)
        qt_ref[c0 + LANES:c0 + QK_PAD, :] = (
            (qm_t[c0 + LANES:c0 + QK_PAD, :] * cos_t + qr_t[r0:r0 + LANES, :] * sin_t) * scale
        ).astype(BF16)
        k_ref[:, c0:c0 + LANES] = k_nope[:, r0:r0 + LANES].astype(BF16)
        k_ref[:, c0 + LANES:c0 + QK_PAD] = kpe
        v0 = hd * V_AUG
        vt_ref[v0:v0 + V_HEAD_DIM, :] = v_t[r0:r0 + V_HEAD_DIM, :].astype(BF16)
        vt_ref[v0 + V_HEAD_DIM:v0 + V_AUG, :] = ones_rows


def _attn_prep(x, g, wa, qn, wqm_t, wqr_t, kvn, wk, wv_t, cos_p, sin_p, *, tm=256):
    b, s, d = x.shape
    scale = float((QK_NOPE_DIM + QK_ROPE_DIM) ** -0.5 * 1.4426950408889634)
    const = lambda bi, i: (0, 0)
    return pl.pallas_call(
        functools.partial(_attn_prep_kernel, scale=scale),
        out_shape=(jax.ShapeDtypeStruct((b, N_HEADS * QK_PAD, s), BF16),
                   jax.ShapeDtypeStruct((b, s, N_HEADS * QK_PAD), BF16),
                   jax.ShapeDtypeStruct((b, N_HEADS * V_AUG, s), BF16)),
        grid=(b, s // tm),
        in_specs=[
            pl.BlockSpec((None, tm, d), lambda bi, i: (bi, i, 0)),
            pl.BlockSpec((1, d), const),
            _resident(wa.shape, const),
            pl.BlockSpec((1, Q_LORA_RANK), const),
            _resident(wqm_t.shape, const),
            _resident(wqr_t.shape, const),
            pl.BlockSpec((1, KV_LORA_RANK), const),
            _resident(wk.shape, const),
            _resident(wv_t.shape, const),
            pl.BlockSpec((tm, LANES), lambda bi, i: (i, 0)),
            pl.BlockSpec((tm, LANES), lambda bi, i: (i, 0)),
            pl.BlockSpec((LANES, tm), lambda bi, i: (0, i)),
            pl.BlockSpec((LANES, tm), lambda bi, i: (0, i)),
        ],
        out_specs=(pl.BlockSpec((None, N_HEADS * QK_PAD, tm), lambda bi, i: (bi, 0, i)),
                   pl.BlockSpec((None, tm, N_HEADS * QK_PAD), lambda bi, i: (bi, i, 0)),
                   pl.BlockSpec((None, N_HEADS * V_AUG, tm), lambda bi, i: (bi, 0, i))),
        compiler_params=_params("parallel", "parallel"),
        name="attn_prep",
    )(x, g, wa, qn, wqm_t, wqr_t, kvn, wk, wv_t, cos_p, sin_p, cos_p.T, sin_p.T)


def _attn_kernel(qt_ref, k_ref, vt_ref, o_ref, acc_ref, s_buf, p_buf, *, tk):
    n_chunks = k_ref.shape[0] // tk
    n_pairs = n_chunks // 2
    tq = qt_ref.shape[1]

    def scores(c, slot):
        start = pl.multiple_of(c * tk, tk)
        s = _dot(k_ref[pl.ds(start, tk), :], qt_ref[...])
        s_buf[slot] = s
        return jnp.max(s, axis=0, keepdims=True)

    def softmax(slot, chunk_max, m_prev):
        m_new = jnp.maximum(m_prev, chunk_max)
        alpha = jnp.exp2(m_prev - m_new)
        p_buf[slot] = jnp.exp2((s_buf[slot] - m_new).astype(BF16))
        return m_new, alpha

    def accumulate(c, slot, alpha):
        start = pl.multiple_of(c * tk, tk)
        pv = _dot(vt_ref[:, pl.ds(start, tk)], p_buf[slot])
        if alpha is None:
            acc_ref[...] = pv
        else:
            acc_ref[...] = alpha * acc_ref[...] + pv

    def pair(c0, m, alpha_prev, cm_a, *, first=False, last=False):
        cm_b = scores(c0 + 1, 1)
        m, alpha_a = softmax(0, cm_a, m)
        if not first:
            accumulate(c0 - 1, 1, alpha_prev)
        if not last:
            cm_a = scores(c0 + 2, 0)
        m, alpha_b = softmax(1, cm_b, m)
        accumulate(c0, 0, None if first else alpha_a)
        return m, alpha_b, cm_a

    m_init = jnp.full((1, tq), -jnp.inf, F32)
    carry = pair(0, m_init, None, scores(0, 0), first=True)
    carry = lax.fori_loop(1, n_pairs - 1, lambda i, c: pair(2 * i, *c), carry)
    _, alpha_last, _ = pair(n_chunks - 2, *carry, last=True)
    accumulate(n_chunks - 1, 1, alpha_last)
    acc = acc_ref[...]
    o = acc[:V_HEAD_DIM, :] / acc[V_HEAD_DIM:V_HEAD_DIM + 1, :]
    o_ref[...] = o.T.astype(o_ref.dtype)


def _attention(qt, k, vt, *, tq=2048, tk=512):
    b, s, _ = k.shape
    assert (s // tk) % 2 == 0 and s // tk >= 4 and s % tq == 0
    return pl.pallas_call(
        functools.partial(_attn_kernel, tk=tk),
        out_shape=jax.ShapeDtypeStruct((b, s, N_HEADS * V_HEAD_DIM), BF16),
        grid=(b, N_HEADS, s // tq),
        in_specs=[
            pl.BlockSpec((None, QK_PAD, tq), lambda bi, h, i: (bi, h, i)),
            pl.BlockSpec((None, s, QK_PAD), lambda bi, h, i: (bi, 0, h)),
            pl.BlockSpec((None, V_AUG, s), lambda bi, h, i: (bi, h, 0)),
        ],
        out_specs=pl.BlockSpec((None, tq, V_HEAD_DIM), lambda bi, h, i: (bi, i, h)),
        scratch_shapes=[pltpu.VMEM((V_AUG, tq), F32),
                        pltpu.VMEM((2, tk, tq), F32),
                        pltpu.VMEM((2, tk, tq), BF16)],
        compiler_params=_params("parallel", "parallel", "arbitrary"),
        name="attention",
    )(qt, k, vt)


def _rec_proj_kernel(x_ref, g_ref, w_ref, o_ref, h_ref):
    j = pl.program_id(1)

    @pl.when(j == 0)
    def _():
        h_ref[...] = _rms(x_ref[...], g_ref[...]).astype(BF16)
        o_ref[...] = _dot(h_ref[...], w_ref[...])

    @pl.when(j == 1)
    def _():
        o_ref[...] = jax.nn.gelu(_dot(h_ref[...], w_ref[...]))


def _rec_proj(x, g, w, *, tm=512):
    m, d = x.shape
    return pl.pallas_call(
        _rec_proj_kernel,
        out_shape=jax.ShapeDtypeStruct((m, 2 * LRU_WIDTH), F32),
        grid=(m // tm, 2),
        in_specs=[
            pl.BlockSpec((tm, d), lambda i, j: (i, 0)),
            pl.BlockSpec((1, d), lambda i, j: (0, 0)),
            pl.BlockSpec((d, LRU_WIDTH), lambda i, j: (0, j)),
        ],
        out_specs=pl.BlockSpec((tm, LRU_WIDTH), lambda i, j: (i, j)),
        scratch_shapes=[pltpu.VMEM((tm, d), BF16)],
        compiler_params=_params("parallel", "arbitrary"),
        name="rec_proj",
    )(x, g, w)


def _lru_kernel(x_ref, gg_ref, cw_ref, cb_ref, w_ref, b_ref, lam_ref, o_ref,
                xpad_ref, af_ref, uf_ref, ab_ref, ub_ref, *, seq, chunk, sub, pitch):
    halo = SUBLANES
    zeros_halo = jnp.zeros((halo, LANES), F32)
    xpad_ref[0:halo, :] = zeros_halo
    xpad_ref[halo + seq:halo + seq + halo, :] = zeros_halo
    xpad_ref[halo:halo + seq, :] = x_ref[...]

    lam = lam_ref[...]
    nl = -lam
    softplus = jnp.maximum(nl, 0.0) + jnp.log1p(jnp.exp(-jnp.abs(nl)))
    log_a_slope = (-0.5 * LRU_C) * softplus
    cw = cw_ref[...]
    cb = cb_ref[...]
    half_bias = b_ref[...]
    a_refs = (af_ref, ab_ref)
    u_refs = (uf_ref, ub_ref)

    def gates(sc, carry):
        t0 = pl.multiple_of(sc * sub, sub)
        xc = cb
        for kk in range(CONV_WIDTH):
            xc = xc + xpad_ref[pl.ds(t0 + halo - CONV_LEFT + kk, sub), :] * cw[kk:kk + 1, :]
        half_z = _dot(xc.astype(BF16), w_ref[...]) + half_bias
        half_xc = 0.5 * xc
        dst = pl.multiple_of((t0 // chunk) * pitch + t0 % chunk, SUBLANES)
        for d in range(2):
            tanh_r = jnp.tanh(half_z[:, 2 * d * LANES:(2 * d + 1) * LANES])
            tanh_i = jnp.tanh(half_z[:, (2 * d + 1) * LANES:(2 * d + 2) * LANES])
            log_a = (tanh_r + 1.0) * log_a_slope[:, d * LANES:(d + 1) * LANES]
            a = jnp.exp(log_a)
            one_minus_a2 = jnp.tanh(log_a) * (-1.0 - a * a)
            a_refs[d][pl.ds(dst, sub), :] = a
            u_refs[d][pl.ds(dst, sub), :] = jnp.sqrt(one_minus_a2) * ((tanh_i + 1.0) * half_xc)
        return carry

    lax.fori_loop(0, seq // sub, gates, 0, unroll=2)

    def strided(ref, grp, j):
        return ref.at[pl.ds(grp * SUBLANES * pitch + j, SUBLANES, stride=pitch), :]

    groups = range(SCAN_GROUPS)
    z = jnp.zeros((SUBLANES, LANES), F32)
    one = jnp.ones((SUBLANES, LANES), F32)

    def pass_a(j, carry):
        hf, pf, hb, pb = carry
        jb = chunk - 1 - j
        af = [strided(af_ref, g, j)[...] for g in groups]
        ab = [strided(ab_ref, g, jb)[...] for g in groups]
        hf = tuple(af[g] * hf[g] + strided(uf_ref, g, j)[...] for g in groups)
        pf = tuple(af[g] * pf[g] for g in groups)
        hb = tuple(ab[g] * hb[g] + strided(ub_ref, g, jb)[...] for g in groups)
        pb = tuple(ab[g] * pb[g] for g in groups)
        return hf, pf, hb, pb

    zs = tuple(z for _ in groups)
    ones = tuple(one for _ in groups)
    hf_e, pf_e, hb_e, pb_e = lax.fori_loop(0, chunk, pass_a, (zs, ones, zs, ones), unroll=8)

    row = lax.broadcasted_iota(jnp.int32, (SUBLANES, LANES), 0)
    order = [(g, c) for g in groups for c in range(SUBLANES)]
    hin_f = [z for _ in groups]
    state = jnp.zeros((1, LANES), F32)
    for g, c in order:
        hin_f[g] = jnp.where(row == c, state, hin_f[g])
        state = hf_e[g][c:c + 1, :] + pf_e[g][c:c + 1, :] * state
    hin_b = [z for _ in groups]
    state = jnp.zeros((1, LANES), F32)
    for g, c in reversed(order):
        hin_b[g] = jnp.where(row == c, state, hin_b[g])
        state = hb_e[g][c:c + 1, :] + pb_e[g][c:c + 1, :] * state

    def pass_b(j, carry):
        hf, hb = carry
        jb = chunk - 1 - j
        hf = tuple(strided(af_ref, g, j)[...] * hf[g] + strided(uf_ref, g, j)[...] for g in groups)
        hb = tuple(strided(ab_ref, g, jb)[...] * hb[g] + strided(ub_ref, g, jb)[...] for g in groups)
        for g in groups:
            strided(uf_ref, g, j)[...] = hf[g]
            strided(ub_ref, g, jb)[...] = hb[g]
        return hf, hb

    lax.fori_loop(0, chunk, pass_b, (tuple(hin_f), tuple(hin_b)), unroll=8)

    def combine(sc, carry):
        t0 = pl.multiple_of(sc * sub, sub)
        src = pl.multiple_of((t0 // chunk) * pitch + t0 % chunk, SUBLANES)
        h_rec = uf_ref[pl.ds(src, sub), :] + ub_ref[pl.ds(src, sub), :]
        o_ref[pl.ds(t0, sub), :] = (gg_ref[pl.ds(t0, sub), :] * h_rec).astype(o_ref.dtype)
        return carry

    lax.fori_loop(0, seq // sub, combine, 0)


def _lru(xg, cw, cb, wcat, bcat, lamcat):
    b, s, _ = xg.shape
    n_chunks = SUBLANES * SCAN_GROUPS
    chunk = s // n_chunks
    sub = min(256, chunk)
    assert s % n_chunks == 0 and chunk % sub == 0 and sub % SUBLANES == 0
    pitch = chunk + SUBLANES if (chunk // SUBLANES) % 2 == 0 else chunk
    scan = pltpu.VMEM((n_chunks * pitch, LANES), F32)
    return pl.pallas_call(
        functools.partial(_lru_kernel, seq=s, chunk=chunk, sub=sub, pitch=pitch),
        out_shape=jax.ShapeDtypeStruct((b, s, LRU_WIDTH), BF16),
        grid=(b, LRU_BLOCKS),
        in_specs=[
            pl.BlockSpec((None, s, LANES), lambda bi, n: (bi, 0, n)),
            pl.BlockSpec((None, s, LANES), lambda bi, n: (bi, 0, LRU_BLOCKS + n)),
            pl.BlockSpec((CONV_WIDTH, LANES), lambda bi, n: (0, n)),
            pl.BlockSpec((1, LANES), lambda bi, n: (0, n)),
            pl.BlockSpec((None, LRU_BLOCK_W, 4 * LRU_BLOCK_W), lambda bi, n: (n, 0, 0)),
            pl.BlockSpec((None, 1, 4 * LRU_BLOCK_W), lambda bi, n: (n, 0, 0)),
            pl.BlockSpec((None, 1, 2 * LRU_BLOCK_W), lambda bi, n: (n, 0, 0)),
        ],
        out_specs=pl.BlockSpec((None, s, LANES), lambda bi, n: (bi, 0, n)),
        scratch_shapes=[pltpu.VMEM((s + 2 * SUBLANES, LANES), F32), scan, scan, scan, scan],
        compiler_params=_params("parallel", "parallel"),
        name="rglru",
    )(xg, xg, cw, cb, wcat, bcat, lamcat)


def _merge_kernel(x_ref, g_ref, oa_ref, yr_ref, wga_ref, wgr_ref, woa_ref, wor_ref, wout_ref,
                  o_ref, h_ref):
    j = pl.program_id(1)

    @pl.when(j == 0)
    def _():
        x = x_ref[...]
        h_ref[...] = _rms(x, g_ref[...]).astype(BF16)
        o_ref[...] = x

    h = h_ref[...]
    ga = _sigmoid(_dot(h, wga_ref[...]))
    gr = _sigmoid(_dot(h, wgr_ref[...]))
    ya = _dot(oa_ref[...], woa_ref[...])
    yr = _dot(yr_ref[...], wor_ref[...])
    merged = (ga * ya + gr * yr).astype(BF16)
    o_ref[...] += _dot(merged, wout_ref[...])


def _merge(x, g, oa, yr, wga, wgr, woa, wor, wout, *, tm=512, tn=512):
    m, d = x.shape
    row = lambda i, j: (i, 0)
    col = lambda i, j: (0, j)
    return pl.pallas_call(
        _merge_kernel,
        out_shape=jax.ShapeDtypeStruct((m, d), F32),
        grid=(m // tm, d // tn),
        in_specs=[
            pl.BlockSpec((tm, d), row),
            pl.BlockSpec((1, d), lambda i, j: (0, 0)),
            pl.BlockSpec((tm, d), row),
            pl.BlockSpec((tm, d), row),
            pl.BlockSpec((d, tn), col),
            pl.BlockSpec((d, tn), col),
            pl.BlockSpec((d, tn), col),
            pl.BlockSpec((d, tn), col),
            pl.BlockSpec((tn, d), lambda i, j: (j, 0)),
        ],
        out_specs=pl.BlockSpec((tm, d), row),
        scratch_shapes=[pltpu.VMEM((tm, d), BF16)],
        compiler_params=_params("parallel", "arbitrary"),
        name="merge",
    )(x, g, oa, yr, wga, wgr, woa, wor, wout)


def _swap_halves(w):
    half = QK_ROPE_DIM // 2
    return jnp.concatenate([-w[..., half:], w[..., :half]], axis=-1)


def _prep_weights(p):
    row = lambda v: v.reshape(1, -1).astype(F32)
    w_in = p["w_in"]
    o = 0
    w_cq = w_in[:, o:o + Q_LORA_RANK]; o += Q_LORA_RANK
    w_ckv = w_in[:, o:o + KV_LORA_RANK]; o += KV_LORA_RANK
    w_kpe = w_in[:, o:o + QK_ROPE_DIM]; o += QK_ROPE_DIM
    w_rec = w_in[:, o:o + LRU_WIDTH]; o += LRU_WIDTH
    w_g = w_in[:, o:o + LRU_WIDTH]; o += LRU_WIDTH
    w_ga = w_in[:, o:o + D_MODEL]; o += D_MODEL
    w_gr = w_in[:, o:o + D_MODEL]

    zpad = jnp.zeros((D_MODEL, LANES - QK_ROPE_DIM), F32)
    wa = jnp.concatenate([w_cq, w_ckv, w_kpe, zpad, _swap_halves(w_kpe), zpad], axis=1)

    w_uq = p["w_uq"].reshape(Q_LORA_RANK, N_HEADS, QK_NOPE_DIM + QK_ROPE_DIM)
    uq_nope, uq_pe = w_uq[..., :QK_NOPE_DIM], w_uq[..., QK_NOPE_DIM:]
    zq = jnp.zeros((Q_LORA_RANK, N_HEADS, LANES - QK_ROPE_DIM), F32)
    wqm = jnp.concatenate([uq_nope, uq_pe, zq], axis=-1).reshape(Q_LORA_RANK, N_HEADS * QK_PAD)
    wqr = jnp.concatenate([_swap_halves(uq_pe), zq], axis=-1).reshape(Q_LORA_RANK, N_HEADS * LANES)

    w_ukv = p["w_ukv"].reshape(KV_LORA_RANK, N_HEADS, QK_NOPE_DIM + V_HEAD_DIM)
    wk = w_ukv[..., :QK_NOPE_DIM].reshape(KV_LORA_RANK, -1)
    wv = w_ukv[..., QK_NOPE_DIM:].reshape(KV_LORA_RANK, -1)

    wa_, wi_ = p["rg_w_a"], p["rg_w_i"]
    wcat = 0.5 * jnp.concatenate([wa_[0], wi_[0], wa_[1], wi_[1]], axis=-1)
    blk = lambda v: v.reshape(LRU_BLOCKS, 1, LRU_BLOCK_W)
    ba, bi = p["rg_b_a"], p["rg_b_i"]
    bcat = 0.5 * jnp.concatenate([blk(ba[0]), blk(bi[0]), blk(ba[1]), blk(bi[1])], axis=-1)
    lam = p["rg_lambda"]
    lamcat = jnp.concatenate([blk(lam[0]), blk(lam[1])], axis=-1)

    bf = lambda v: v.astype(BF16)
    return dict(
        ffn1=(row(p["ffn1_norm"]), bf(p["ffn1_w_gate"]), bf(p["ffn1_w_up"]), bf(p["ffn1_w_down"])),
        ffn2=(row(p["ffn2_norm"]), bf(p["ffn2_w_gate"]), bf(p["ffn2_w_up"]), bf(p["ffn2_w_down"])),
        mix_norm=row(p["mix_norm"]), wa=bf(wa), q_norm=row(p["q_norm"]),
        wqm_t=bf(wqm.T), wqr_t=bf(wqr.T), kv_norm=row(p["kv_norm"]), wk=bf(wk), wv_t=bf(wv.T),
        w_recg=bf(jnp.concatenate([w_rec, w_g], axis=1)),
        conv_w=p["conv_w"].astype(F32), conv_b=row(p["conv_b"]),
        wcat=bf(wcat), bcat=bcat.astype(F32), lamcat=lamcat.astype(F32),
        w_ga=bf(w_ga), w_gr=bf(w_gr), w_o_attn=bf(p["w_o_attn"]), w_o_rec=bf(p["w_o_rec"]),
        w_out=bf(p["w_out"]), final_norm=row(p["final_norm"]),
    )


def _rope_tables(seq):
    pos = jnp.arange(seq, dtype=F32)
    inv_freq = ROPE_BASE ** (-jnp.arange(0, QK_ROPE_DIM, 2, dtype=F32) / QK_ROPE_DIM)
    ang = pos[:, None] * inv_freq[None, :]
    pad = jnp.zeros((seq, LANES - QK_ROPE_DIM), F32)
    cos, sin = jnp.cos(ang), jnp.sin(ang)
    return (jnp.concatenate([cos, cos, pad], axis=1), jnp.concatenate([sin, sin, pad], axis=1))


def _encoder(x, w, cos_p, sin_p):
    b, s, d = x.shape
    xf = x.reshape(b * s, d)
    x1 = _ffn(xf, *w["ffn1"], w["final_norm"], final_norm=False)
    qt, k, vt = _attn_prep(x1.reshape(b, s, d), w["mix_norm"], w["wa"], w["q_norm"], w["wqm_t"],
                           w["wqr_t"], w["kv_norm"], w["wk"], w["wv_t"], cos_p, sin_p)
    o_attn = _attention(qt, k, vt)
    xg = _rec_proj(x1, w["mix_norm"], w["w_recg"])
    y_rec = _lru(xg.reshape(b, s, -1), w["conv_w"], w["conv_b"], w["wcat"], w["bcat"], w["lamcat"])
    x2 = _merge(x1, w["mix_norm"], o_attn.reshape(b * s, -1), y_rec.reshape(b * s, -1),
                w["w_ga"], w["w_gr"], w["w_o_attn"], w["w_o_rec"], w["w_out"])
    y = _ffn(x2, *w["ffn2"], w["final_norm"], final_norm=True)
    return y.reshape(b, s, d)


def kernel(x_prompt, x_sample, ffn1_norm, ffn1_w_gate, ffn1_w_up, ffn1_w_down, mix_norm, w_in, q_norm, w_uq, kv_norm, w_ukv, w_o_attn, conv_w, conv_b, rg_w_a, rg_b_a, rg_w_i, rg_b_i, rg_lambda, w_o_rec, w_out, ffn2_norm, ffn2_w_gate, ffn2_w_up, ffn2_w_down, final_norm):
    layer0 = dict(
        ffn1_norm=ffn1_norm[0], ffn1_w_gate=ffn1_w_gate[0], ffn1_w_up=ffn1_w_up[0],
        ffn1_w_down=ffn1_w_down[0], mix_norm=mix_norm[0], w_in=w_in[0], q_norm=q_norm[0],
        w_uq=w_uq[0], kv_norm=kv_norm[0], w_ukv=w_ukv[0], w_o_attn=w_o_attn[0],
        conv_w=conv_w[0], conv_b=conv_b[0], rg_w_a=rg_w_a[0], rg_b_a=rg_b_a[0],
        rg_w_i=rg_w_i[0], rg_b_i=rg_b_i[0], rg_lambda=rg_lambda[0], w_o_rec=w_o_rec[0],
        w_out=w_out[0], ffn2_norm=ffn2_norm[0], ffn2_w_gate=ffn2_w_gate[0],
        ffn2_w_up=ffn2_w_up[0], ffn2_w_down=ffn2_w_down[0], final_norm=final_norm)
    w = _prep_weights(layer0)
    assert x_prompt.shape[1] == x_sample.shape[1]
    cos_p, sin_p = _rope_tables(x_prompt.shape[1])
    return (_encoder(x_prompt, w, cos_p, sin_p), _encoder(x_sample, w, cos_p, sin_p))
```

```python
import functools

import jax
import jax.numpy as jnp
from jax import lax
from jax.experimental import pallas as pl
from jax.experimental.pallas import tpu as pltpu

D_MODEL = 2048
N_HEADS = 16
QK_NOPE_DIM = 128
QK_ROPE_DIM = 64
V_HEAD_DIM = 128
Q_LORA_RANK = 512
KV_LORA_RANK = 512
ROPE_BASE = 10000.0
LRU_WIDTH = D_MODEL
LRU_BLOCKS = 16
LRU_BLOCK_W = LRU_WIDTH // LRU_BLOCKS
CONV_WIDTH = 4
CONV_LEFT = 2
LRU_C = 8.0
RMS_EPS = 1e-6

LANES = 128
SUBLANES = 8
QK_PAD = 256
V_AUG = V_HEAD_DIM + 16
VMEM_LIMIT = 56 * 1024 * 1024
SCAN_GROUPS = 4

BF16 = jnp.bfloat16
F32 = jnp.float32


def _rms(x, g):
    ms = jnp.mean(x * x, axis=-1, keepdims=True)
    return x * lax.rsqrt(ms + RMS_EPS) * g


def _sigmoid(x):
    return 0.5 * jnp.tanh(0.5 * x) + 0.5


def _dot(a, b):
    return jnp.dot(a, b, preferred_element_type=F32)


def _params(*sem):
    return pltpu.CompilerParams(dimension_semantics=sem, vmem_limit_bytes=VMEM_LIMIT)


def _resident(shape, index_map):
    return pl.BlockSpec(shape, index_map, pipeline_mode=pl.Buffered(1))


def _ffn_kernel(x_ref, g_ref, wg_ref, wu_ref, wd_ref, fn_ref, o_ref, h_ref, *, final_norm):
    j = pl.program_id(1)

    @pl.when(j == 0)
    def _():
        x = x_ref[...]
        h_ref[...] = _rms(x, g_ref[...]).astype(BF16)
        o_ref[...] = x

    h = h_ref[...]
    a = _dot(h, wg_ref[...])
    b = _dot(h, wu_ref[...])
    act = (a * _sigmoid(a) * b).astype(BF16)
    o_ref[...] += 0.5 * _dot(act, wd_ref[...])

    if final_norm:
        @pl.when(j == pl.num_programs(1) - 1)
        def _():
            o_ref[...] = _rms(o_ref[...], fn_ref[...])


def _ffn(x, g, wg, wu, wd, fn, *, final_norm, tm=1024, tf=512):
    m, d = x.shape
    ff = wg.shape[1]
    return pl.pallas_call(
        functools.partial(_ffn_kernel, final_norm=final_norm),
        out_shape=jax.ShapeDtypeStruct((m, d), F32),
        grid=(m // tm, ff // tf),
        in_specs=[
            pl.BlockSpec((tm, d), lambda i, j: (i, 0)),
            pl.BlockSpec((1, d), lambda i, j: (0, 0)),
            pl.BlockSpec((d, tf), lambda i, j: (0, j)),
            pl.BlockSpec((d, tf), lambda i, j: (0, j)),
            pl.BlockSpec((tf, d), lambda i, j: (j, 0)),
            pl.BlockSpec((1, d), lambda i, j: (0, 0)),
        ],
        out_specs=pl.BlockSpec((tm, d), lambda i, j: (i, 0)),
        scratch_shapes=[pltpu.VMEM((tm, d), BF16)],
        compiler_params=_params("parallel", "arbitrary"),
        name="ffn_final" if final_norm else "ffn",
    )(x, g, wg, wu, wd, fn)


def _attn_prep_kernel(x_ref, g_ref, wa_ref, qn_ref, wqm_ref, wqr_ref, kvn_ref, wk_ref, wv_ref,
                      cos_ref, sin_ref, cost_ref, sint_ref, qt_ref, k_ref, vt_ref, *, scale):
    h = _rms(x_ref[...], g_ref[...]).astype(BF16)
    p = _dot(h, wa_ref[...])
    cq = p[:, :Q_LORA_RANK]
    ckv = p[:, Q_LORA_RANK:Q_LORA_RANK + KV_LORA_RANK]
    base = Q_LORA_RANK + KV_LORA_RANK
    kp_main = p[:, base:base + LANES]
    kp_rot = p[:, base + LANES:base + 2 * LANES]
    kpe = (kp_main * cos_ref[...] + kp_rot * sin_ref[...]).astype(BF16)
    cqn_t = _rms(cq, qn_ref[...]).T.astype(BF16)
    ckvn = _rms(ckv, kvn_ref[...])
    ckvn_t = ckvn.T.astype(BF16)
    qm_t = _dot(wqm_ref[...], cqn_t)
    qr_t = _dot(wqr_ref[...], cqn_t)
    k_nope = _dot(ckvn.astype(BF16), wk_ref[...])
    v_t = _dot(wv_ref[...], ckvn_t)
    ones_rows = jnp.ones((V_AUG - V_HEAD_DIM, v_t.shape[1]), BF16)
    cos_t = cost_ref[...]
    sin_t = sint_ref[...]
    q_head = QK_NOPE_DIM + QK_ROPE_DIM
    zero_rows = jnp.zeros((QK_PAD - q_head, qm_t.shape[1]), BF16)
    for hd in range(N_HEADS):
        c0 = hd * QK_PAD
        r0 = hd * LANES
        n0 = hd * q_head
        e0 = hd * QK_ROPE_DIM
        qt_ref[c0:c0 + LANES, :] = (qm_t[n0:n0 + QK_NOPE_DIM, :] * scale).astype(BF16)
        qt_ref[c0 + LANES:c0 + q_head, :] = (
            (qm_t[n0 + QK_NOPE_DIM:n0 + q_head, :] * cos_t + qr_t[e0:e0 + QK_ROPE_DIM, :] * sin_t)
            * scale).astype(BF16)
        qt_ref[c0 + q_head:c0 + QK_PAD, :] = zero_rows
        k_ref[:, c0:c0 + LANES] = k_nope[:, r0:r0 + LANES].astype(BF16)
        k_ref[:, c0 + LANES:c0 + QK_PAD] = kpe
        v0 = hd * V_AUG
        vt_ref[v0:v0 + V_HEAD_DIM, :] = v_t[r0:r0 + V_HEAD_DIM, :].astype(BF16)
        vt_ref[v0 + V_HEAD_DIM:v0 + V_AUG, :] = ones_rows


def _attn_prep(x, g, wa, qn, wqm_t, wqr_t, kvn, wk, wv_t, cos_p, sin_p, *, tm=256):
    b, s, d = x.shape
    scale = float((QK_NOPE_DIM + QK_ROPE_DIM) ** -0.5 * 1.4426950408889634)
    const = lambda bi, i: (0, 0)
    return pl.pallas_call(
        functools.partial(_attn_prep_kernel, scale=scale),
        out_shape=(jax.ShapeDtypeStruct((b, N_HEADS * QK_PAD, s), BF16),
                   jax.ShapeDtypeStruct((b, s, N_HEADS * QK_PAD), BF16),
                   jax.ShapeDtypeStruct((b, N_HEADS * V_AUG, s), BF16)),
        grid=(b, s // tm),
        in_specs=[
            pl.BlockSpec((None, tm, d), lambda bi, i: (bi, i, 0)),
            pl.BlockSpec((1, d), const),
            _resident(wa.shape, const),
            pl.BlockSpec((1, Q_LORA_RANK), const),
            _resident(wqm_t.shape, const),
            _resident(wqr_t.shape, const),
            pl.BlockSpec((1, KV_LORA_RANK), const),
            _resident(wk.shape, const),
            _resident(wv_t.shape, const),
            pl.BlockSpec((tm, LANES), lambda bi, i: (i, 0)),
            pl.BlockSpec((tm, LANES), lambda bi, i: (i, 0)),
            pl.BlockSpec((QK_ROPE_DIM, tm), lambda bi, i: (0, i)),
            pl.BlockSpec((QK_ROPE_DIM, tm), lambda bi, i: (0, i)),
        ],
        out_specs=(pl.BlockSpec((None, N_HEADS * QK_PAD, tm), lambda bi, i: (bi, 0, i)),
                   pl.BlockSpec((None, tm, N_HEADS * QK_PAD), lambda bi, i: (bi, i, 0)),
                   pl.BlockSpec((None, N_HEADS * V_AUG, tm), lambda bi, i: (bi, 0, i))),
        compiler_params=_params("parallel", "parallel"),
        name="attn_prep",
    )(x, g, wa, qn, wqm_t, wqr_t, kvn, wk, wv_t, cos_p, sin_p,
      cos_p.T[:QK_ROPE_DIM], sin_p.T[:QK_ROPE_DIM])


def _attn_kernel(qt_ref, k_ref, vt_ref, o_ref, acc_ref, s_buf, p_buf, *, tk):
    n_chunks = k_ref.shape[0] // tk
    n_pairs = n_chunks // 2
    tq = qt_ref.shape[1]

    def scores(c, slot):
        start = pl.multiple_of(c * tk, tk)
        s = _dot(k_ref[pl.ds(start, tk), :], qt_ref[...])
        s_buf[slot] = s
        return jnp.max(s, axis=0, keepdims=True)

    def softmax(slot, chunk_max, m_prev):
        m_new = jnp.maximum(m_prev, chunk_max)
        alpha = jnp.exp2(m_prev - m_new)
        p_buf[slot] = jnp.exp2((s_buf[slot] - m_new).astype(BF16))
        return m_new, alpha

    def accumulate(c, slot, alpha):
        start = pl.multiple_of(c * tk, tk)
        pv = _dot(vt_ref[:, pl.ds(start, tk)], p_buf[slot])
        if alpha is None:
            acc_ref[...] = pv
        else:
            acc_ref[...] = alpha * acc_ref[...] + pv

    def pair(c0, m, alpha_prev, cm_a, *, first=False, last=False):
        cm_b = scores(c0 + 1, 1)
        m, alpha_a = softmax(0, cm_a, m)
        if not first:
            accumulate(c0 - 1, 1, alpha_prev)
        if not last:
            cm_a = scores(c0 + 2, 0)
        m, alpha_b = softmax(1, cm_b, m)
        accumulate(c0, 0, None if first else alpha_a)
        return m, alpha_b, cm_a

    m_init = jnp.full((1, tq), -jnp.inf, F32)
    carry = pair(0, m_init, None, scores(0, 0), first=True)
    carry = lax.fori_loop(1, n_pairs - 1, lambda i, c: pair(2 * i, *c), carry)
    _, alpha_last, _ = pair(n_chunks - 2, *carry, last=True)
    accumulate(n_chunks - 1, 1, alpha_last)
    acc = acc_ref[...]
    o = acc[:V_HEAD_DIM, :] / acc[V_HEAD_DIM:V_HEAD_DIM + 1, :]
    o_ref[...] = o.T.astype(o_ref.dtype)


def _attention(qt, k, vt, *, tq=2048, tk=512):
    b, s, _ = k.shape
    assert (s // tk) % 2 == 0 and s // tk >= 4 and s % tq == 0
    return pl.pallas_call(
        functools.partial(_attn_kernel, tk=tk),
        out_shape=jax.ShapeDtypeStruct((b, s, N_HEADS * V_HEAD_DIM), BF16),
        grid=(b, N_HEADS, s // tq),
        in_specs=[
            pl.BlockSpec((None, QK_PAD, tq), lambda bi, h, i: (bi, h, i)),
            pl.BlockSpec((None, s, QK_PAD), lambda bi, h, i: (bi, 0, h)),
            pl.BlockSpec((None, V_AUG, s), lambda bi, h, i: (bi, h, 0)),
        ],
        out_specs=pl.BlockSpec((None, tq, V_HEAD_DIM), lambda bi, h, i: (bi, i, h)),
        scratch_shapes=[pltpu.VMEM((V_AUG, tq), F32),
                        pltpu.VMEM((2, tk, tq), F32),
                        pltpu.VMEM((2, tk, tq), BF16)],
        compiler_params=_params("parallel", "parallel", "arbitrary"),
        name="attention",
    )(qt, k, vt)


def _rec_proj_kernel(x_ref, g_ref, w_ref, o_ref, h_ref):
    j = pl.program_id(1)

    @pl.when(j == 0)
    def _():
        h_ref[...] = _rms(x_ref[...], g_ref[...]).astype(BF16)
        o_ref[...] = _dot(h_ref[...], w_ref[...])

    @pl.when(j == 1)
    def _():
        o_ref[...] = jax.nn.gelu(_dot(h_ref[...], w_ref[...]))


def _rec_proj(x, g, w, *, tm=512):
    m, d = x.shape
    return pl.pallas_call(
        _rec_proj_kernel,
        out_shape=jax.ShapeDtypeStruct((m, 2 * LRU_WIDTH), F32),
        grid=(m // tm, 2),
        in_specs=[
            pl.BlockSpec((tm, d), lambda i, j: (i, 0)),
            pl.BlockSpec((1, d), lambda i, j: (0, 0)),
            pl.BlockSpec((d, LRU_WIDTH), lambda i, j: (0, j)),
        ],
        out_specs=pl.BlockSpec((tm, LRU_WIDTH), lambda i, j: (i, j)),
        scratch_shapes=[pltpu.VMEM((tm, d), BF16)],
        compiler_params=_params("parallel", "arbitrary"),
        name="rec_proj",
    )(x, g, w)


def _lru_kernel(x_ref, gg_ref, cw_ref, cb_ref, w_ref, b_ref, lam_ref, o_ref,
                xpad_ref, af_ref, uf_ref, ab_ref, ub_ref, *, seq, chunk, sub, pitch):
    halo = SUBLANES
    zeros_halo = jnp.zeros((halo, LANES), F32)
    xpad_ref[0:halo, :] = zeros_halo
    xpad_ref[halo + seq:halo + seq + halo, :] = zeros_halo
    xpad_ref[halo:halo + seq, :] = x_ref[...]

    lam = lam_ref[...]
    nl = -lam
    softplus = jnp.maximum(nl, 0.0) + jnp.log1p(jnp.exp(-jnp.abs(nl)))
    log_a_slope = (-0.5 * LRU_C) * softplus
    cw = cw_ref[...]
    cb = cb_ref[...]
    half_bias = b_ref[...]
    a_refs = (af_ref, ab_ref)
    u_refs = (uf_ref, ub_ref)

    def gates(sc, carry):
        t0 = pl.multiple_of(sc * sub, sub)
        xc = cb
        for kk in range(CONV_WIDTH):
            xc = xc + xpad_ref[pl.ds(t0 + halo - CONV_LEFT + kk, sub), :] * cw[kk:kk + 1, :]
        half_z = _dot(xc.astype(BF16), w_ref[...]) + half_bias
        half_xc = 0.5 * xc
        dst = pl.multiple_of((t0 // chunk) * pitch + t0 % chunk, 4)
        for d in range(2):
            tanh_r = jnp.tanh(half_z[:, 2 * d * LANES:(2 * d + 1) * LANES])
            tanh_i = jnp.tanh(half_z[:, (2 * d + 1) * LANES:(2 * d + 2) * LANES])
            log_a = (tanh_r + 1.0) * log_a_slope[:, d * LANES:(d + 1) * LANES]
            a = jnp.exp(log_a)
            one_minus_a2 = jnp.tanh(log_a) * (-1.0 - a * a)
            a_refs[d][pl.ds(dst, sub), :] = a
            u_refs[d][pl.ds(dst, sub), :] = jnp.sqrt(one_minus_a2) * ((tanh_i + 1.0) * half_xc)
        return carry

    lax.fori_loop(0, seq // sub, gates, 0, unroll=4)

    def strided(ref, grp, j):
        return ref.at[pl.ds(grp * SUBLANES * pitch + j, SUBLANES, stride=pitch), :]

    groups = range(SCAN_GROUPS)
    z = jnp.zeros((SUBLANES, LANES), F32)
    one = jnp.ones((SUBLANES, LANES), F32)

    def pass_a(j, carry):
        hf, pf, hb, pb = carry
        jb = chunk - 1 - j
        af = [strided(af_ref, g, j)[...] for g in groups]
        ab = [strided(ab_ref, g, jb)[...] for g in groups]
        hf = tuple(af[g] * hf[g] + strided(uf_ref, g, j)[...] for g in groups)
        pf = tuple(af[g] * pf[g] for g in groups)
        hb = tuple(ab[g] * hb[g] + strided(ub_ref, g, jb)[...] for g in groups)
        pb = tuple(ab[g] * pb[g] for g in groups)
        return hf, pf, hb, pb

    zs = tuple(z for _ in groups)
    ones = tuple(one for _ in groups)
    hf_e, pf_e, hb_e, pb_e = lax.fori_loop(0, chunk, pass_a, (zs, ones, zs, ones), unroll=8)

    row = lax.broadcasted_iota(jnp.int32, (SUBLANES, LANES), 0)
    order = [(g, c) for g in groups for c in range(SUBLANES)]
    hin_f = [z for _ in groups]
    state = jnp.zeros((1, LANES), F32)
    for g, c in order:
        hin_f[g] = jnp.where(row == c, state, hin_f[g])
        state = hf_e[g][c:c + 1, :] + pf_e[g][c:c + 1, :] * state
    hin_b = [z for _ in groups]
    state = jnp.zeros((1, LANES), F32)
    for g, c in reversed(order):
        hin_b[g] = jnp.where(row == c, state, hin_b[g])
        state = hb_e[g][c:c + 1, :] + pb_e[g][c:c + 1, :] * state

    def pass_b(j, carry):
        hf, hb = carry
        jb = chunk - 1 - j
        hf = tuple(strided(af_ref, g, j)[...] * hf[g] + strided(uf_ref, g, j)[...] for g in groups)
        hb = tuple(strided(ab_ref, g, jb)[...] * hb[g] + strided(ub_ref, g, jb)[...] for g in groups)
        for g in groups:
            strided(uf_ref, g, j)[...] = hf[g]
            strided(ub_ref, g, jb)[...] = hb[g]
        return hf, hb

    lax.fori_loop(0, chunk, pass_b, (tuple(hin_f), tuple(hin_b)), unroll=8)

    def combine(sc, carry):
        t0 = pl.multiple_of(sc * sub, sub)
        src = pl.multiple_of((t0 // chunk) * pitch + t0 % chunk, 4)
        h_rec = uf_ref[pl.ds(src, sub), :] + ub_ref[pl.ds(src, sub), :]
        o_ref[pl.ds(t0, sub), :] = (gg_ref[pl.ds(t0, sub), :] * h_rec).astype(o_ref.dtype)
        return carry

    lax.fori_loop(0, seq // sub, combine, 0)


def _lru(xg, cw, cb, wcat, bcat, lamcat):
    b, s, _ = xg.shape
    n_chunks = SUBLANES * SCAN_GROUPS
    chunk = s // n_chunks
    sub = min(256, chunk)
    assert s % n_chunks == 0 and chunk % sub == 0 and sub % SUBLANES == 0
    pitch = chunk + 4
    scan = pltpu.VMEM((n_chunks * pitch, LANES), F32)
    return pl.pallas_call(
        functools.partial(_lru_kernel, seq=s, chunk=chunk, sub=sub, pitch=pitch),
        out_shape=jax.ShapeDtypeStruct((b, s, LRU_WIDTH), BF16),
        grid=(b, LRU_BLOCKS),
        in_specs=[
            pl.BlockSpec((None, s, LANES), lambda bi, n: (bi, 0, n)),
            pl.BlockSpec((None, s, LANES), lambda bi, n: (bi, 0, LRU_BLOCKS + n)),
            pl.BlockSpec((CONV_WIDTH, LANES), lambda bi, n: (0, n)),
            pl.BlockSpec((1, LANES), lambda bi, n: (0, n)),
            pl.BlockSpec((None, LRU_BLOCK_W, 4 * LRU_BLOCK_W), lambda bi, n: (n, 0, 0)),
            pl.BlockSpec((None, 1, 4 * LRU_BLOCK_W), lambda bi, n: (n, 0, 0)),
            pl.BlockSpec((None, 1, 2 * LRU_BLOCK_W), lambda bi, n: (n, 0, 0)),
        ],
        out_specs=pl.BlockSpec((None, s, LANES), lambda bi, n: (bi, 0, n)),
        scratch_shapes=[pltpu.VMEM((s + 2 * SUBLANES, LANES), F32), scan, scan, scan, scan],
        compiler_params=_params("parallel", "parallel"),
        name="rglru",
    )(xg, xg, cw, cb, wcat, bcat, lamcat)


def _merge_kernel(x_ref, g_ref, oa_ref, yr_ref, wga_ref, wgr_ref, woa_ref, wor_ref, wout_ref,
                  o_ref, h_ref):
    j = pl.program_id(1)

    @pl.when(j == 0)
    def _():
        x = x_ref[...]
        h_ref[...] = _rms(x, g_ref[...]).astype(BF16)
        o_ref[...] = x

    h = h_ref[...]
    ga = _sigmoid(_dot(h, wga_ref[...]))
    gr = _sigmoid(_dot(h, wgr_ref[...]))
    ya = _dot(oa_ref[...], woa_ref[...])
    yr = _dot(yr_ref[...], wor_ref[...])
    merged = (ga * ya + gr * yr).astype(BF16)
    o_ref[...] += _dot(merged, wout_ref[...])


def _merge(x, g, oa, yr, wga, wgr, woa, wor, wout, *, tm=512, tn=512):
    m, d = x.shape
    row = lambda i, j: (i, 0)
    col = lambda i, j: (0, j)
    return pl.pallas_call(
        _merge_kernel,
        out_shape=jax.ShapeDtypeStruct((m, d), F32),
        grid=(m // tm, d // tn),
        in_specs=[
            pl.BlockSpec((tm, d), row),
            pl.BlockSpec((1, d), lambda i, j: (0, 0)),
            pl.BlockSpec((tm, d), row),
            pl.BlockSpec((tm, d), row),
            pl.BlockSpec((d, tn), col),
            pl.BlockSpec((d, tn), col),
            pl.BlockSpec((d, tn), col),
            pl.BlockSpec((d, tn), col),
            pl.BlockSpec((tn, d), lambda i, j: (j, 0)),
        ],
        out_specs=pl.BlockSpec((tm, d), row),
        scratch_shapes=[pltpu.VMEM((tm, d), BF16)],
        compiler_params=_params("parallel", "arbitrary"),
        name="merge",
    )(x, g, oa, yr, wga, wgr, woa, wor, wout)


def _swap_halves(w):
    half = QK_ROPE_DIM // 2
    return jnp.concatenate([-w[..., half:], w[..., :half]], axis=-1)


def _prep_weights(p):
    row = lambda v: v.reshape(1, -1).astype(F32)
    w_in = p["w_in"]
    o = 0
    w_cq = w_in[:, o:o + Q_LORA_RANK]; o += Q_LORA_RANK
    w_ckv = w_in[:, o:o + KV_LORA_RANK]; o += KV_LORA_RANK
    w_kpe = w_in[:, o:o + QK_ROPE_DIM]; o += QK_ROPE_DIM
    w_rec = w_in[:, o:o + LRU_WIDTH]; o += LRU_WIDTH
    w_g = w_in[:, o:o + LRU_WIDTH]; o += LRU_WIDTH
    w_ga = w_in[:, o:o + D_MODEL]; o += D_MODEL
    w_gr = w_in[:, o:o + D_MODEL]

    zpad = jnp.zeros((D_MODEL, LANES - QK_ROPE_DIM), F32)
    wa = jnp.concatenate([w_cq, w_ckv, w_kpe, zpad, _swap_halves(w_kpe), zpad], axis=1)

    w_uq = p["w_uq"].reshape(Q_LORA_RANK, N_HEADS, QK_NOPE_DIM + QK_ROPE_DIM)
    wqm = p["w_uq"]
    wqr = _swap_halves(w_uq[..., QK_NOPE_DIM:]).reshape(Q_LORA_RANK, N_HEADS * QK_ROPE_DIM)

    w_ukv = p["w_ukv"].reshape(KV_LORA_RANK, N_HEADS, QK_NOPE_DIM + V_HEAD_DIM)
    wk = w_ukv[..., :QK_NOPE_DIM].reshape(KV_LORA_RANK, -1)
    wv = w_ukv[..., QK_NOPE_DIM:].reshape(KV_LORA_RANK, -1)

    wa_, wi_ = p["rg_w_a"], p["rg_w_i"]
    wcat = 0.5 * jnp.concatenate([wa_[0], wi_[0], wa_[1], wi_[1]], axis=-1)
    blk = lambda v: v.reshape(LRU_BLOCKS, 1, LRU_BLOCK_W)
    ba, bi = p["rg_b_a"], p["rg_b_i"]
    bcat = 0.5 * jnp.concatenate([blk(ba[0]), blk(bi[0]), blk(ba[1]), blk(bi[1])], axis=-1)
    lam = p["rg_lambda"]
    lamcat = jnp.concatenate([blk(lam[0]), blk(lam[1])], axis=-1)

    bf = lambda v: v.astype(BF16)
    return dict(
        ffn1=(row(p["ffn1_norm"]), bf(p["ffn1_w_gate"]), bf(p["ffn1_w_up"]), bf(p["ffn1_w_down"])),
        ffn2=(row(p["ffn2_norm"]), bf(p["ffn2_w_gate"]), bf(p["ffn2_w_up"]), bf(p["ffn2_w_down"])),
        mix_norm=row(p["mix_norm"]), wa=bf(wa), q_norm=row(p["q_norm"]),
        wqm_t=bf(wqm.T), wqr_t=bf(wqr.T), kv_norm=row(p["kv_norm"]), wk=bf(wk), wv_t=bf(wv.T),
        w_recg=bf(jnp.concatenate([w_rec, w_g], axis=1)),
        conv_w=p["conv_w"].astype(F32), conv_b=row(p["conv_b"]),
        wcat=bf(wcat), bcat=bcat.astype(F32), lamcat=lamcat.astype(F32),
        w_ga=bf(w_ga), w_gr=bf(w_gr), w_o_attn=bf(p["w_o_attn"]), w_o_rec=bf(p["w_o_rec"]),
        w_out=bf(p["w_out"]), final_norm=row(p["final_norm"]),
    )


def _rope_tables(seq):
    pos = jnp.arange(seq, dtype=F32)
    inv_freq = ROPE_BASE ** (-jnp.arange(0, QK_ROPE_DIM, 2, dtype=F32) / QK_ROPE_DIM)
    ang = pos[:, None] * inv_freq[None, :]
    pad = jnp.zeros((seq, LANES - QK_ROPE_DIM), F32)
    cos, sin = jnp.cos(ang), jnp.sin(ang)
    return (jnp.concatenate([cos, cos, pad], axis=1), jnp.concatenate([sin, sin, pad], axis=1))


def _encoder(x, w, cos_p, sin_p):
    b, s, d = x.shape
    xf = x.reshape(b * s, d)
    x1 = _ffn(xf, *w["ffn1"], w["final_norm"], final_norm=False)
    qt, k, vt = _attn_prep(x1.reshape(b, s, d), w["mix_norm"], w["wa"], w["q_norm"], w["wqm_t"],
                           w["wqr_t"], w["kv_norm"], w["wk"], w["wv_t"], cos_p, sin_p)
    o_attn = _attention(qt, k, vt)
    xg = _rec_proj(x1, w["mix_norm"], w["w_recg"])
    y_rec = _lru(xg.reshape(b, s, -1), w["conv_w"], w["conv_b"], w["wcat"], w["bcat"], w["lamcat"])
    x2 = _merge(x1, w["mix_norm"], o_attn.reshape(b * s, -1), y_rec.reshape(b * s, -1),
                w["w_ga"], w["w_gr"], w["w_o_attn"], w["w_o_rec"], w["w_out"])
    y = _ffn(x2, *w["ffn2"], w["final_norm"], final_norm=True)
    return y.reshape(b, s, d)


def kernel(x_prompt, x_sample, ffn1_norm, ffn1_w_gate, ffn1_w_up, ffn1_w_down, mix_norm, w_in, q_norm, w_uq, kv_norm, w_ukv, w_o_attn, conv_w, conv_b, rg_w_a, rg_b_a, rg_w_i, rg_b_i, rg_lambda, w_o_rec, w_out, ffn2_norm, ffn2_w_gate, ffn2_w_up, ffn2_w_down, final_norm):
    layer0 = dict(
        ffn1_norm=ffn1_norm[0], ffn1_w_gate=ffn1_w_gate[0], ffn1_w_up=ffn1_w_up[0],
        ffn1_w_down=ffn1_w_down[0], mix_norm=mix_norm[0], w_in=w_in[0], q_norm=q_norm[0],
        w_uq=w_uq[0], kv_norm=kv_norm[0], w_ukv=w_ukv[0], w_o_attn=w_o_attn[0],
        conv_w=conv_w[0], conv_b=conv_b[0], rg_w_a=rg_w_a[0], rg_b_a=rg_b_a[0],
        rg_w_i=rg_w_i[0], rg_b_i=rg_b_i[0], rg_lambda=rg_lambda[0], w_o_rec=w_o_rec[0],
        w_out=w_out[0], ffn2_norm=ffn2_norm[0], ffn2_w_gate=ffn2_w_gate[0],
        ffn2_w_up=ffn2_w_up[0], ffn2_w_down=ffn2_w_down[0], final_norm=final_norm)
    w = _prep_weights(layer0)
    assert x_prompt.shape[1] == x_sample.shape[1]
    cos_p, sin_p = _rope_tables(x_prompt.shape[1])
    return (_encoder(x_prompt, w, cos_p, sin_p), _encoder(x_sample, w, cos_p, sin_p))
```

```python
import functools

import jax
import jax.numpy as jnp
from jax import lax
from jax.experimental import pallas as pl
from jax.experimental.pallas import tpu as pltpu

D_MODEL = 2048
N_HEADS = 16
QK_NOPE_DIM = 128
QK_ROPE_DIM = 64
V_HEAD_DIM = 128
Q_LORA_RANK = 512
KV_LORA_RANK = 512
ROPE_BASE = 10000.0
LRU_WIDTH = D_MODEL
LRU_BLOCKS = 16
LRU_BLOCK_W = LRU_WIDTH // LRU_BLOCKS
CONV_WIDTH = 4
CONV_LEFT = 2
LRU_C = 8.0
RMS_EPS = 1e-6

LANES = 128
SUBLANES = 8
QK_PAD = 256
V_AUG = V_HEAD_DIM + 16
VMEM_LIMIT = 56 * 1024 * 1024
SCAN_GROUPS = 4
SCAN_PITCH_PAD = 4

BF16 = jnp.bfloat16
F32 = jnp.float32


def _rms(x, g):
    ms = jnp.mean(x * x, axis=-1, keepdims=True)
    return x * lax.rsqrt(ms + RMS_EPS) * g


def _sigmoid(x):
    return 0.5 * jnp.tanh(0.5 * x) + 0.5


def _dot(a, b):
    return jnp.dot(a, b, preferred_element_type=F32)


def _params(*sem):
    return pltpu.CompilerParams(dimension_semantics=sem, vmem_limit_bytes=VMEM_LIMIT)


def _resident(shape, index_map):
    return pl.BlockSpec(shape, index_map, pipeline_mode=pl.Buffered(1))


def _ffn_kernel(x_ref, g_ref, wg_ref, wu_ref, wd_ref, fn_ref, o_ref, h_ref, *, final_norm):
    j = pl.program_id(1)

    @pl.when(j == 0)
    def _():
        x = x_ref[...]
        h_ref[...] = _rms(x, g_ref[...]).astype(BF16)
        o_ref[...] = x

    h = h_ref[...]
    a = _dot(h, wg_ref[...])
    b = _dot(h, wu_ref[...])
    act = (a * _sigmoid(a) * b).astype(BF16)
    o_ref[...] += 0.5 * _dot(act, wd_ref[...])

    if final_norm:
        @pl.when(j == pl.num_programs(1) - 1)
        def _():
            o_ref[...] = _rms(o_ref[...], fn_ref[...])


def _ffn(x, g, wg, wu, wd, fn, *, final_norm, tm=1024, tf=512):
    m, d = x.shape
    ff = wg.shape[1]
    return pl.pallas_call(
        functools.partial(_ffn_kernel, final_norm=final_norm),
        out_shape=jax.ShapeDtypeStruct((m, d), F32),
        grid=(m // tm, ff // tf),
        in_specs=[
            pl.BlockSpec((tm, d), lambda i, j: (i, 0)),
            pl.BlockSpec((1, d), lambda i, j: (0, 0)),
            pl.BlockSpec((d, tf), lambda i, j: (0, j)),
            pl.BlockSpec((d, tf), lambda i, j: (0, j)),
            pl.BlockSpec((tf, d), lambda i, j: (j, 0)),
            pl.BlockSpec((1, d), lambda i, j: (0, 0)),
        ],
        out_specs=pl.BlockSpec((tm, d), lambda i, j: (i, 0)),
        scratch_shapes=[pltpu.VMEM((tm, d), BF16)],
        compiler_params=_params("parallel", "arbitrary"),
        name="ffn_final" if final_norm else "ffn",
    )(x, g, wg, wu, wd, fn)


def _attn_prep_kernel(x_ref, g_ref, wa_ref, qn_ref, wqm_ref, wqr_ref, kvn_ref, wk_ref, wv_ref,
                      cos_ref, sin_ref, cost_ref, sint_ref, qt_ref, k_ref, vt_ref, *, scale):
    h = _rms(x_ref[...], g_ref[...]).astype(BF16)
    p = _dot(h, wa_ref[...])
    cq = p[:, :Q_LORA_RANK]
    ckv = p[:, Q_LORA_RANK:Q_LORA_RANK + KV_LORA_RANK]
    base = Q_LORA_RANK + KV_LORA_RANK
    kp_main = p[:, base:base + LANES]
    kp_rot = p[:, base + LANES:base + 2 * LANES]
    kpe = (kp_main * cos_ref[...] + kp_rot * sin_ref[...]).astype(BF16)
    cqn_t = _rms(cq, qn_ref[...]).T.astype(BF16)
    ckvn = _rms(ckv, kvn_ref[...])
    ckvn_t = ckvn.T.astype(BF16)
    qm_t = _dot(wqm_ref[...], cqn_t)
    qr_t = _dot(wqr_ref[...], cqn_t)
    k_nope = _dot(ckvn.astype(BF16), wk_ref[...])
    v_t = _dot(wv_ref[...], ckvn_t)
    ones_rows = jnp.ones((V_AUG - V_HEAD_DIM, v_t.shape[1]), BF16)
    cos_t = cost_ref[...]
    sin_t = sint_ref[...]
    q_head = QK_NOPE_DIM + QK_ROPE_DIM
    zero_rows = jnp.zeros((QK_PAD - q_head, qm_t.shape[1]), BF16)
    for hd in range(N_HEADS):
        c0 = hd * QK_PAD
        r0 = hd * LANES
        n0 = hd * q_head
        e0 = hd * QK_ROPE_DIM
        qt_ref[c0:c0 + LANES, :] = (qm_t[n0:n0 + QK_NOPE_DIM, :] * scale).astype(BF16)
        qt_ref[c0 + LANES:c0 + q_head, :] = (
            (qm_t[n0 + QK_NOPE_DIM:n0 + q_head, :] * cos_t + qr_t[e0:e0 + QK_ROPE_DIM, :] * sin_t)
            * scale).astype(BF16)
        qt_ref[c0 + q_head:c0 + QK_PAD, :] = zero_rows
        k_ref[:, c0:c0 + LANES] = k_nope[:, r0:r0 + LANES].astype(BF16)
        k_ref[:, c0 + LANES:c0 + QK_PAD] = kpe
        v0 = hd * V_AUG
        vt_ref[v0:v0 + V_HEAD_DIM, :] = v_t[r0:r0 + V_HEAD_DIM, :].astype(BF16)
        vt_ref[v0 + V_HEAD_DIM:v0 + V_AUG, :] = ones_rows


def _attn_prep(x, g, wa, qn, wqm_t, wqr_t, kvn, wk, wv_t, cos_p, sin_p, *, tm=256):
    b, s, d = x.shape
    scale = float((QK_NOPE_DIM + QK_ROPE_DIM) ** -0.5 * 1.4426950408889634)
    const = lambda bi, i: (0, 0)
    return pl.pallas_call(
        functools.partial(_attn_prep_kernel, scale=scale),
        out_shape=(jax.ShapeDtypeStruct((b, N_HEADS * QK_PAD, s), BF16),
                   jax.ShapeDtypeStruct((b, s, N_HEADS * QK_PAD), BF16),
                   jax.ShapeDtypeStruct((b, N_HEADS * V_AUG, s), BF16)),
        grid=(b, s // tm),
        in_specs=[
            pl.BlockSpec((None, tm, d), lambda bi, i: (bi, i, 0)),
            pl.BlockSpec((1, d), const),
            _resident(wa.shape, const),
            pl.BlockSpec((1, Q_LORA_RANK), const),
            _resident(wqm_t.shape, const),
            _resident(wqr_t.shape, const),
            pl.BlockSpec((1, KV_LORA_RANK), const),
            _resident(wk.shape, const),
            _resident(wv_t.shape, const),
            pl.BlockSpec((tm, LANES), lambda bi, i: (i, 0)),
            pl.BlockSpec((tm, LANES), lambda bi, i: (i, 0)),
            pl.BlockSpec((QK_ROPE_DIM, tm), lambda bi, i: (0, i)),
            pl.BlockSpec((QK_ROPE_DIM, tm), lambda bi, i: (0, i)),
        ],
        out_specs=(pl.BlockSpec((None, N_HEADS * QK_PAD, tm), lambda bi, i: (bi, 0, i)),
                   pl.BlockSpec((None, tm, N_HEADS * QK_PAD), lambda bi, i: (bi, i, 0)),
                   pl.BlockSpec((None, N_HEADS * V_AUG, tm), lambda bi, i: (bi, 0, i))),
        compiler_params=_params("parallel", "parallel"),
        name="attn_prep",
    )(x, g, wa, qn, wqm_t, wqr_t, kvn, wk, wv_t, cos_p, sin_p,
      cos_p.T[:QK_ROPE_DIM], sin_p.T[:QK_ROPE_DIM])


def _attn_kernel(qt_ref, k_ref, vt_ref, o_ref, acc_ref, s_buf, p_buf, *, tk):
    n_chunks = k_ref.shape[0] // tk
    n_pairs = n_chunks // 2
    tq = qt_ref.shape[1]

    def scores(c, slot):
        start = pl.multiple_of(c * tk, tk)
        s = _dot(k_ref[pl.ds(start, tk), :], qt_ref[...])
        s_buf[slot] = s
        return jnp.max(s, axis=0, keepdims=True)

    def softmax(slot, chunk_max, m_prev):
        m_new = jnp.maximum(m_prev, chunk_max)
        alpha = jnp.exp2(m_prev - m_new)
        p_buf[slot] = jnp.exp2((s_buf[slot] - m_new).astype(BF16))
        return m_new, alpha

    def accumulate(c, slot, alpha):
        start = pl.multiple_of(c * tk, tk)
        pv = _dot(vt_ref[:, pl.ds(start, tk)], p_buf[slot])
        if alpha is None:
            acc_ref[...] = pv
        else:
            acc_ref[...] = alpha * acc_ref[...] + pv

    def pair(c0, m, alpha_prev, cm_a, *, first=False, last=False):
        cm_b = scores(c0 + 1, 1)
        m, alpha_a = softmax(0, cm_a, m)
        if not first:
            accumulate(c0 - 1, 1, alpha_prev)
        if not last:
            cm_a = scores(c0 + 2, 0)
        m, alpha_b = softmax(1, cm_b, m)
        accumulate(c0, 0, None if first else alpha_a)
        return m, alpha_b, cm_a

    m_init = jnp.full((1, tq), -jnp.inf, F32)
    carry = pair(0, m_init, None, scores(0, 0), first=True)
    carry = lax.fori_loop(1, n_pairs - 1, lambda i, c: pair(2 * i, *c), carry)
    _, alpha_last, _ = pair(n_chunks - 2, *carry, last=True)
    accumulate(n_chunks - 1, 1, alpha_last)
    acc = acc_ref[...]
    o = acc[:V_HEAD_DIM, :] / acc[V_HEAD_DIM:V_HEAD_DIM + 1, :]
    o_ref[...] = o.T.astype(o_ref.dtype)


def _attention(qt, k, vt, *, tq=2048, tk=512):
    b, s, _ = k.shape
    assert (s // tk) % 2 == 0 and s // tk >= 4 and s % tq == 0
    return pl.pallas_call(
        functools.partial(_attn_kernel, tk=tk),
        out_shape=jax.ShapeDtypeStruct((b, s, N_HEADS * V_HEAD_DIM), BF16),
        grid=(b, N_HEADS, s // tq),
        in_specs=[
            pl.BlockSpec((None, QK_PAD, tq), lambda bi, h, i: (bi, h, i)),
            pl.BlockSpec((None, s, QK_PAD), lambda bi, h, i: (bi, 0, h)),
            pl.BlockSpec((None, V_AUG, s), lambda bi, h, i: (bi, h, 0)),
        ],
        out_specs=pl.BlockSpec((None, tq, V_HEAD_DIM), lambda bi, h, i: (bi, i, h)),
        scratch_shapes=[pltpu.VMEM((V_AUG, tq), F32),
                        pltpu.VMEM((2, tk, tq), F32),
                        pltpu.VMEM((2, tk, tq), BF16)],
        compiler_params=_params("parallel", "parallel", "arbitrary"),
        name="attention",
    )(qt, k, vt)


def _rec_proj_kernel(x_ref, g_ref, w_ref, o_ref, h_ref):
    j = pl.program_id(1)

    @pl.when(j == 0)
    def _():
        h_ref[...] = _rms(x_ref[...], g_ref[...]).astype(BF16)
        o_ref[...] = _dot(h_ref[...], w_ref[...])

    @pl.when(j == 1)
    def _():
        o_ref[...] = jax.nn.gelu(_dot(h_ref[...], w_ref[...]))


def _rec_proj(x, g, w, *, tm=512):
    m, d = x.shape
    return pl.pallas_call(
        _rec_proj_kernel,
        out_shape=jax.ShapeDtypeStruct((m, 2 * LRU_WIDTH), F32),
        grid=(m // tm, 2),
        in_specs=[
            pl.BlockSpec((tm, d), lambda i, j: (i, 0)),
            pl.BlockSpec((1, d), lambda i, j: (0, 0)),
            pl.BlockSpec((d, LRU_WIDTH), lambda i, j: (0, j)),
        ],
        out_specs=pl.BlockSpec((tm, LRU_WIDTH), lambda i, j: (i, j)),
        scratch_shapes=[pltpu.VMEM((tm, d), BF16)],
        compiler_params=_params("parallel", "arbitrary"),
        name="rec_proj",
    )(x, g, w)


def _lru_kernel(x_ref, gg_ref, cw_ref, cb_ref, w_ref, b_ref, lam_ref, o_ref,
                xpad_ref, af_ref, uf_ref, ab_ref, ub_ref, *, seq, chunk, sub, pitch):
    halo = SUBLANES
    zeros_halo = jnp.zeros((halo, LANES), F32)
    xpad_ref[0:halo, :] = zeros_halo
    xpad_ref[halo + seq:halo + seq + halo, :] = zeros_halo
    xpad_ref[halo:halo + seq, :] = x_ref[...]

    lam = lam_ref[...]
    nl = -lam
    softplus = jnp.maximum(nl, 0.0) + jnp.log1p(jnp.exp(-jnp.abs(nl)))
    log_a_slope = (-0.5 * LRU_C) * softplus
    cw = cw_ref[...]
    cb = cb_ref[...]
    half_bias = b_ref[...]
    a_refs = (af_ref, ab_ref)
    u_refs = (uf_ref, ub_ref)

    def gates(sc, carry):
        t0 = pl.multiple_of(sc * sub, sub)
        xc = cb
        for kk in range(CONV_WIDTH):
            xc = xc + xpad_ref[pl.ds(t0 + halo - CONV_LEFT + kk, sub), :] * cw[kk:kk + 1, :]
        half_z = _dot(xc.astype(BF16), w_ref[...]) + half_bias
        half_xc = 0.5 * xc
        dst = pl.multiple_of((t0 // chunk) * pitch + t0 % chunk, SCAN_PITCH_PAD)
        for d in range(2):
            tanh_r = jnp.tanh(half_z[:, 2 * d * LANES:(2 * d + 1) * LANES])
            tanh_i = jnp.tanh(half_z[:, (2 * d + 1) * LANES:(2 * d + 2) * LANES])
            log_a = (tanh_r + 1.0) * log_a_slope[:, d * LANES:(d + 1) * LANES]
            a = jnp.exp(log_a)
            one_minus_a2 = jnp.tanh(log_a) * (-1.0 - a * a)
            a_refs[d][pl.ds(dst, sub), :] = a
            u_refs[d][pl.ds(dst, sub), :] = jnp.sqrt(one_minus_a2) * ((tanh_i + 1.0) * half_xc)
        return carry

    lax.fori_loop(0, seq // sub, gates, 0, unroll=4)

    def strided(ref, grp, j):
        return ref.at[pl.ds(grp * SUBLANES * pitch + j, SUBLANES, stride=pitch), :]

    groups = range(SCAN_GROUPS)
    z = jnp.zeros((SUBLANES, LANES), F32)
    one = jnp.ones((SUBLANES, LANES), F32)

    def pass_a(j, carry):
        hf, pf, hb, pb = carry
        jb = chunk - 1 - j
        af = [strided(af_ref, g, j)[...] for g in groups]
        ab = [strided(ab_ref, g, jb)[...] for g in groups]
        hf = tuple(af[g] * hf[g] + strided(uf_ref, g, j)[...] for g in groups)
        pf = tuple(af[g] * pf[g] for g in groups)
        hb = tuple(ab[g] * hb[g] + strided(ub_ref, g, jb)[...] for g in groups)
        pb = tuple(ab[g] * pb[g] for g in groups)
        return hf, pf, hb, pb

    zs = tuple(z for _ in groups)
    ones = tuple(one for _ in groups)
    hf_e, pf_e, hb_e, pb_e = lax.fori_loop(0, chunk, pass_a, (zs, ones, zs, ones), unroll=8)

    row = lax.broadcasted_iota(jnp.int32, (SUBLANES, LANES), 0)
    order = [(g, c) for g in groups for c in range(SUBLANES)]
    hin_f = [z for _ in groups]
    state = jnp.zeros((1, LANES), F32)
    for g, c in order:
        hin_f[g] = jnp.where(row == c, state, hin_f[g])
        state = hf_e[g][c:c + 1, :] + pf_e[g][c:c + 1, :] * state
    hin_b = [z for _ in groups]
    state = jnp.zeros((1, LANES), F32)
    for g, c in reversed(order):
        hin_b[g] = jnp.where(row == c, state, hin_b[g])
        state = hb_e[g][c:c + 1, :] + pb_e[g][c:c + 1, :] * state

    def pass_b(j, carry):
        hf, hb = carry
        jb = chunk - 1 - j
        hf = tuple(strided(af_ref, g, j)[...] * hf[g] + strided(uf_ref, g, j)[...] for g in groups)
        hb = tuple(strided(ab_ref, g, jb)[...] * hb[g] + strided(ub_ref, g, jb)[...] for g in groups)
        for g in groups:
            strided(uf_ref, g, j)[...] = hf[g]
            strided(ub_ref, g, jb)[...] = hb[g]
        return hf, hb

    lax.fori_loop(0, chunk, pass_b, (tuple(hin_f), tuple(hin_b)), unroll=8)

    def combine(sc, carry):
        t0 = pl.multiple_of(sc * sub, sub)
        src = pl.multiple_of((t0 // chunk) * pitch + t0 % chunk, SCAN_PITCH_PAD)
        h_rec = uf_ref[pl.ds(src, sub), :] + ub_ref[pl.ds(src, sub), :]
        o_ref[pl.ds(t0, sub), :] = (gg_ref[pl.ds(t0, sub), :] * h_rec).astype(o_ref.dtype)
        return carry

    lax.fori_loop(0, seq // sub, combine, 0)


def _lru(xg, cw, cb, wcat, bcat, lamcat):
    b, s, _ = xg.shape
    n_chunks = SUBLANES * SCAN_GROUPS
    chunk = s // n_chunks
    sub = min(256, chunk)
    assert s % n_chunks == 0 and chunk % sub == 0 and sub % SUBLANES == 0
    assert chunk % (2 * SCAN_PITCH_PAD) == 0
    pitch = chunk + SCAN_PITCH_PAD
    scan = pltpu.VMEM((n_chunks * pitch, LANES), F32)
    return pl.pallas_call(
        functools.partial(_lru_kernel, seq=s, chunk=chunk, sub=sub, pitch=pitch),
        out_shape=jax.ShapeDtypeStruct((b, s, LRU_WIDTH), BF16),
        grid=(b, LRU_BLOCKS),
        in_specs=[
            pl.BlockSpec((None, s, LANES), lambda bi, n: (bi, 0, n)),
            pl.BlockSpec((None, s, LANES), lambda bi, n: (bi, 0, LRU_BLOCKS + n)),
            pl.BlockSpec((CONV_WIDTH, LANES), lambda bi, n: (0, n)),
            pl.BlockSpec((1, LANES), lambda bi, n: (0, n)),
            pl.BlockSpec((None, LRU_BLOCK_W, 4 * LRU_BLOCK_W), lambda bi, n: (n, 0, 0)),
            pl.BlockSpec((None, 1, 4 * LRU_BLOCK_W), lambda bi, n: (n, 0, 0)),
            pl.BlockSpec((None, 1, 2 * LRU_BLOCK_W), lambda bi, n: (n, 0, 0)),
        ],
        out_specs=pl.BlockSpec((None, s, LANES), lambda bi, n: (bi, 0, n)),
        scratch_shapes=[pltpu.VMEM((s + 2 * SUBLANES, LANES), F32), scan, scan, scan, scan],
        compiler_params=_params("parallel", "parallel"),
        name="rglru",
    )(xg, xg, cw, cb, wcat, bcat, lamcat)


def _merge_kernel(x_ref, g_ref, oa_ref, yr_ref, wga_ref, wgr_ref, woa_ref, wor_ref, wout_ref,
                  o_ref, h_ref):
    j = pl.program_id(1)

    @pl.when(j == 0)
    def _():
        x = x_ref[...]
        h_ref[...] = _rms(x, g_ref[...]).astype(BF16)
        o_ref[...] = x

    h = h_ref[...]
    ga = _sigmoid(_dot(h, wga_ref[...]))
    gr = _sigmoid(_dot(h, wgr_ref[...]))
    ya = _dot(oa_ref[...], woa_ref[...])
    yr = _dot(yr_ref[...], wor_ref[...])
    merged = (ga * ya + gr * yr).astype(BF16)
    o_ref[...] += _dot(merged, wout_ref[...])


def _merge(x, g, oa, yr, wga, wgr, woa, wor, wout, *, tm=512, tn=512):
    m, d = x.shape
    row = lambda i, j: (i, 0)
    col = lambda i, j: (0, j)
    return pl.pallas_call(
        _merge_kernel,
        out_shape=jax.ShapeDtypeStruct((m, d), F32),
        grid=(m // tm, d // tn),
        in_specs=[
            pl.BlockSpec((tm, d), row),
            pl.BlockSpec((1, d), lambda i, j: (0, 0)),
            pl.BlockSpec((tm, d), row),
            pl.BlockSpec((tm, d), row),
            pl.BlockSpec((d, tn), col),
            pl.BlockSpec((d, tn), col),
            pl.BlockSpec((d, tn), col),
            pl.BlockSpec((d, tn), col),
            pl.BlockSpec((tn, d), lambda i, j: (j, 0)),
        ],
        out_specs=pl.BlockSpec((tm, d), row),
        scratch_shapes=[pltpu.VMEM((tm, d), BF16)],
        compiler_params=_params("parallel", "arbitrary"),
        name="merge",
    )(x, g, oa, yr, wga, wgr, woa, wor, wout)


def _swap_halves(w):
    half = QK_ROPE_DIM // 2
    return jnp.concatenate([-w[..., half:], w[..., :half]], axis=-1)


def _prep_weights(p):
    row = lambda v: v.reshape(1, -1).astype(F32)
    w_in = p["w_in"]
    o = 0
    w_cq = w_in[:, o:o + Q_LORA_RANK]; o += Q_LORA_RANK
    w_ckv = w_in[:, o:o + KV_LORA_RANK]; o += KV_LORA_RANK
    w_kpe = w_in[:, o:o + QK_ROPE_DIM]; o += QK_ROPE_DIM
    w_rec = w_in[:, o:o + LRU_WIDTH]; o += LRU_WIDTH
    w_g = w_in[:, o:o + LRU_WIDTH]; o += LRU_WIDTH
    w_ga = w_in[:, o:o + D_MODEL]; o += D_MODEL
    w_gr = w_in[:, o:o + D_MODEL]

    zpad = jnp.zeros((D_MODEL, LANES - QK_ROPE_DIM), F32)
    wa = jnp.concatenate([w_cq, w_ckv, w_kpe, zpad, _swap_halves(w_kpe), zpad], axis=1)

    w_uq = p["w_uq"].reshape(Q_LORA_RANK, N_HEADS, QK_NOPE_DIM + QK_ROPE_DIM)
    wqm = p["w_uq"]
    wqr = _swap_halves(w_uq[..., QK_NOPE_DIM:]).reshape(Q_LORA_RANK, N_HEADS * QK_ROPE_DIM)

    w_ukv = p["w_ukv"].reshape(KV_LORA_RANK, N_HEADS, QK_NOPE_DIM + V_HEAD_DIM)
    wk = w_ukv[..., :QK_NOPE_DIM].reshape(KV_LORA_RANK, -1)
    wv = w_ukv[..., QK_NOPE_DIM:].reshape(KV_LORA_RANK, -1)

    wa_, wi_ = p["rg_w_a"], p["rg_w_i"]
    wcat = 0.5 * jnp.concatenate([wa_[0], wi_[0], wa_[1], wi_[1]], axis=-1)
    blk = lambda v: v.reshape(LRU_BLOCKS, 1, LRU_BLOCK_W)
    ba, bi = p["rg_b_a"], p["rg_b_i"]
    bcat = 0.5 * jnp.concatenate([blk(ba[0]), blk(bi[0]), blk(ba[1]), blk(bi[1])], axis=-1)
    lam = p["rg_lambda"]
    lamcat = jnp.concatenate([blk(lam[0]), blk(lam[1])], axis=-1)

    bf = lambda v: v.astype(BF16)
    return dict(
        ffn1=(row(p["ffn1_norm"]), bf(p["ffn1_w_gate"]), bf(p["ffn1_w_up"]), bf(p["ffn1_w_down"])),
        ffn2=(row(p["ffn2_norm"]), bf(p["ffn2_w_gate"]), bf(p["ffn2_w_up"]), bf(p["ffn2_w_down"])),
        mix_norm=row(p["mix_norm"]), wa=bf(wa), q_norm=row(p["q_norm"]),
        wqm_t=bf(wqm.T), wqr_t=bf(wqr.T), kv_norm=row(p["kv_norm"]), wk=bf(wk), wv_t=bf(wv.T),
        w_recg=bf(jnp.concatenate([w_rec, w_g], axis=1)),
        conv_w=p["conv_w"].astype(F32), conv_b=row(p["conv_b"]),
        wcat=bf(wcat), bcat=bcat.astype(F32), lamcat=lamcat.astype(F32),
        w_ga=bf(w_ga), w_gr=bf(w_gr), w_o_attn=bf(p["w_o_attn"]), w_o_rec=bf(p["w_o_rec"]),
        w_out=bf(p["w_out"]), final_norm=row(p["final_norm"]),
    )


def _rope_tables(seq):
    pos = jnp.arange(seq, dtype=F32)
    inv_freq = ROPE_BASE ** (-jnp.arange(0, QK_ROPE_DIM, 2, dtype=F32) / QK_ROPE_DIM)
    ang = pos[:, None] * inv_freq[None, :]
    pad = jnp.zeros((seq, LANES - QK_ROPE_DIM), F32)
    cos, sin = jnp.cos(ang), jnp.sin(ang)
    return (jnp.concatenate([cos, cos, pad], axis=1), jnp.concatenate([sin, sin, pad], axis=1))


def _encoder(x, w, cos_p, sin_p):
    b, s, d = x.shape
    xf = x.reshape(b * s, d)
    x1 = _ffn(xf, *w["ffn1"], w["final_norm"], final_norm=False)
    qt, k, vt = _attn_prep(x1.reshape(b, s, d), w["mix_norm"], w["wa"], w["q_norm"], w["wqm_t"],
                           w["wqr_t"], w["kv_norm"], w["wk"], w["wv_t"], cos_p, sin_p)
    o_attn = _attention(qt, k, vt)
    xg = _rec_proj(x1, w["mix_norm"], w["w_recg"])
    y_rec = _lru(xg.reshape(b, s, -1), w["conv_w"], w["conv_b"], w["wcat"], w["bcat"], w["lamcat"])
    x2 = _merge(x1, w["mix_norm"], o_attn.reshape(b * s, -1), y_rec.reshape(b * s, -1),
                w["w_ga"], w["w_gr"], w["w_o_attn"], w["w_o_rec"], w["w_out"])
    y = _ffn(x2, *w["ffn2"], w["final_norm"], final_norm=True)
    return y.reshape(b, s, d)


def kernel(x_prompt, x_sample, ffn1_norm, ffn1_w_gate, ffn1_w_up, ffn1_w_down, mix_norm, w_in, q_norm, w_uq, kv_norm, w_ukv, w_o_attn, conv_w, conv_b, rg_w_a, rg_b_a, rg_w_i, rg_b_i, rg_lambda, w_o_rec, w_out, ffn2_norm, ffn2_w_gate, ffn2_w_up, ffn2_w_down, final_norm):
    layer0 = dict(
        ffn1_norm=ffn1_norm[0], ffn1_w_gate=ffn1_w_gate[0], ffn1_w_up=ffn1_w_up[0],
        ffn1_w_down=ffn1_w_down[0], mix_norm=mix_norm[0], w_in=w_in[0], q_norm=q_norm[0],
        w_uq=w_uq[0], kv_norm=kv_norm[0], w_ukv=w_ukv[0], w_o_attn=w_o_attn[0],
        conv_w=conv_w[0], conv_b=conv_b[0], rg_w_a=rg_w_a[0], rg_b_a=rg_b_a[0],
        rg_w_i=rg_w_i[0], rg_b_i=rg_b_i[0], rg_lambda=rg_lambda[0], w_o_rec=w_o_rec[0],
        w_out=w_out[0], ffn2_norm=ffn2_norm[0], ffn2_w_gate=ffn2_w_gate[0],
        ffn2_w_up=ffn2_w_up[0], ffn2_w_down=ffn2_w_down[0], final_norm=final_norm)
    w = _prep_weights(layer0)
    assert x_prompt.shape[1] == x_sample.shape[1]
    cos_p, sin_p = _rope_tables(x_prompt.shape[1])
    return (_encoder(x_prompt, w, cos_p, sin_p), _encoder(x_sample, w, cos_p, sin_p))
```
